```python
import jax, jax.numpy as jnp
from jax import lax
import numpy as np

D_MODEL = 1024
BATCH = 8
SEQ = 2048
DEPTH = 4

GRID_W = 64
CTX_LEN = 256
HEAD_DIM = 64
ROPE_THETA = 10000.0
EPS = 1e-6
Q_BLOCK = 128

GLA_HEADS = D_MODEL // 128
GLA_DK = 32
GLA_DV = 64
GLA_RANK = 16
GLA_TAU = 16.0
GLA_CHUNK = 64
GLA_QK_W = GLA_HEADS * GLA_DK
GLA_V_W = GLA_HEADS * GLA_DV
B_HEADS = D_MODEL // 128
B_KV_HEADS = 2
C_HEADS = D_MODEL // HEAD_DIM
C_KV_HEADS = 2
WINDOW = 128
FFN_HIDDEN = -(-8 * D_MODEL // (3 * 256)) * 256

EVEN_WIDTHS = (GLA_QK_W, GLA_QK_W, GLA_V_W, GLA_V_W, GLA_RANK, GLA_RANK,
               B_HEADS * HEAD_DIM, B_KV_HEADS * HEAD_DIM, B_KV_HEADS * HEAD_DIM)
EVEN_IN = sum(EVEN_WIDTHS)
EVEN_MIX = GLA_V_W + B_HEADS * HEAD_DIM
ODD_WIDTHS = (C_HEADS * HEAD_DIM, C_KV_HEADS * HEAD_DIM, C_KV_HEADS * HEAD_DIM)
ODD_IN = sum(ODD_WIDTHS)
ODD_MIX = C_HEADS * HEAD_DIM
N_EVEN = (DEPTH + 1) // 2
N_ODD = DEPTH // 2

kernel_name = "hybrid_gla_gqa_swa_prefix_dit"


def rms_norm(x, g):
    xf = x.astype(jnp.float32)
    y = xf * lax.rsqrt(jnp.mean(xf * xf, axis=-1, keepdims=True) + EPS)
    return y.astype(x.dtype) * g


def modulate(h, shift, scale):
    return h * (1 + scale) + shift


def split_cols(t, widths):
    idx = np.cumsum(np.array(widths))[:-1].tolist()
    return jnp.split(t, idx, axis=-1)


def heads(t, n_heads):
    return t.reshape(t.shape[0], t.shape[1], n_heads, -1)


def rope_tables(n_tokens, dtype):
    rows = n_tokens // GRID_W
    r, cidx = jnp.meshgrid(jnp.arange(rows), jnp.arange(GRID_W), indexing="ij")
    row = r.reshape(-1).astype(jnp.float32)
    col = cidx.reshape(-1).astype(jnp.float32)
    half = HEAD_DIM // 2
    inv = ROPE_THETA ** (-jnp.arange(0, half, 2, dtype=jnp.float32) / half)
    ang_r = row[:, None] * inv[None]
    ang_c = col[:, None] * inv[None]
    ang = jnp.concatenate([ang_r, ang_r, ang_c, ang_c], axis=-1)
    return jnp.cos(ang)[:, None, :].astype(dtype), jnp.sin(ang)[:, None, :].astype(dtype)


def apply_rope(x, cos, sin):
    q = HEAD_DIM // 4
    x0, x1, x2, x3 = x[..., :q], x[..., q:2 * q], x[..., 2 * q:3 * q], x[..., 3 * q:]
    rot = jnp.concatenate([-x1, x0, -x3, x2], axis=-1)
    return x * cos + rot * sin


def gla_chunk_scan(q, k, v, gk, s0):
    bn, L, H, _ = q.shape
    dv = v.shape[-1]
    nc = L // GLA_CHUNK

    def to_chunks(t):
        return t.astype(jnp.float32).reshape(bn, nc, GLA_CHUNK, H, t.shape[-1]).transpose(1, 0, 3, 2, 4)

    lower = jnp.tril(jnp.ones((GLA_CHUNK, GLA_CHUNK), dtype=bool))

    def step(S, inp):
        qc, kc, vc, gc = inp
        b = jnp.cumsum(gc, axis=2)
        o_inter = jnp.einsum("bhik,bhkv->bhiv", qc * jnp.exp(b), S)
        diff = jnp.where(lower[:, :, None], b[:, :, :, None, :] - b[:, :, None, :, :], -jnp.inf)
        A = jnp.einsum("bhik,bhjk,bhijk->bhij", qc, kc, jnp.exp(diff))
        o_intra = jnp.einsum("bhij,bhjv->bhiv", A, vc)
        b_last = b[:, :, -1]
        S = S * jnp.exp(b_last)[..., None] + jnp.einsum(
            "bhjk,bhjv->bhkv", kc * jnp.exp(b_last[:, :, None] - b), vc)
        return S, o_intra + o_inter

    S, o = lax.scan(step, s0, (to_chunks(q), to_chunks(k), to_chunks(v), to_chunks(gk)))
    o = o.transpose(1, 0, 3, 2, 4).reshape(bn, L, H, dv)
    return o, S


def dense_block_attention(q, k, v):
    bn, L, H, D = q.shape
    hkv = k.shape[2]
    nb = L // Q_BLOCK
    qb = q.reshape(bn, nb, Q_BLOCK, hkv, H // hkv, D).transpose(1, 0, 2, 3, 4, 5)
    scale = D ** -0.5

    def one(qblk):
        s = jnp.einsum("bqhgd,bkhd->bhgqk", qblk, k).astype(jnp.float32) * scale
        p = jax.nn.softmax(s, axis=-1).astype(v.dtype)
        return jnp.einsum("bhgqk,bkhd->bqhgd", p, v)

    o = lax.map(one, qb)
    return o.transpose(1, 0, 2, 3, 4, 5).reshape(bn, L, H * D)


def window_sink_attention(q, k, v, kc, vc, sink):
    bn, L, H, D = q.shape
    hkv = k.shape[2]
    g = H // hkv
    nb = L // Q_BLOCK
    band = Q_BLOCK + 2 * WINDOW
    pad = ((0, 0), (WINDOW, WINDOW), (0, 0), (0, 0))
    kp, vp = jnp.pad(k, pad), jnp.pad(v, pad)
    qb = q.reshape(bn, nb, Q_BLOCK, hkv, g, D).transpose(1, 0, 2, 3, 4, 5)
    sink_l = sink.reshape(hkv, g).astype(jnp.float32)
    scale = D ** -0.5
    s_ctx_len = kc.shape[1]

    def one(args):
        n, qblk = args
        start = n * Q_BLOCK
        kb = lax.dynamic_slice_in_dim(kp, start, band, axis=1)
        vb = lax.dynamic_slice_in_dim(vp, start, band, axis=1)
        qpos = start + jnp.arange(Q_BLOCK)
        kpos = start - WINDOW + jnp.arange(band)
        valid = (jnp.abs(qpos[:, None] - kpos[None, :]) <= WINDOW) & (kpos >= 0)[None] & (kpos < L)[None]
        s_loc = jnp.einsum("bqhgd,bkhd->bhgqk", qblk, kb).astype(jnp.float32) * scale
        s_loc = jnp.where(valid, s_loc, -jnp.inf)
        s_ctx = jnp.einsum("bqhgd,bkhd->bhgqk", qblk, kc).astype(jnp.float32) * scale
        s_sink = jnp.broadcast_to(sink_l[None, :, :, None, None], s_loc.shape[:-1] + (1,))
        p = jax.nn.softmax(jnp.concatenate([s_loc, s_ctx, s_sink], axis=-1), axis=-1).astype(v.dtype)
        return (jnp.einsum("bhgqk,bkhd->bqhgd", p[..., :band], vb)
                + jnp.einsum("bhgqk,bkhd->bqhgd", p[..., band:band + s_ctx_len], vc))

    o = lax.map(one, (jnp.arange(nb), qb))
    return o.transpose(1, 0, 2, 3, 4, 5).reshape(bn, L, H * D)


def ctx_sink_attention(q, k, v, sink):
    bn, Lc, H, D = q.shape
    hkv = k.shape[2]
    g = H // hkv
    qg = q.reshape(bn, Lc, hkv, g, D)
    s = jnp.einsum("bqhgd,bkhd->bhgqk", qg, k).astype(jnp.float32) * D ** -0.5
    s_sink = jnp.broadcast_to(sink.reshape(hkv, g).astype(jnp.float32)[None, :, :, None, None], s.shape[:-1] + (1,))
    p = jax.nn.softmax(jnp.concatenate([s, s_sink], axis=-1), axis=-1)[..., :Lc].astype(v.dtype)
    return jnp.einsum("bhgqk,bkhd->bqhgd", p, v).reshape(bn, Lc, H * D)


def even_mixer(hx, hc, w_in, w_out, gk_w, gk_b, gla_g, qn_g, kn_g, cos, sin, with_ctx):
    def project(h):
        gq, gkk, gv, gg, lr_f, lr_b, bq, bk, bv = split_cols(h @ w_in, EVEN_WIDTHS)
        dec_f = jax.nn.log_sigmoid((lr_f @ gk_w[0] + gk_b[0]).astype(jnp.float32)) / GLA_TAU
        dec_b = jax.nn.log_sigmoid((lr_b @ gk_w[1] + gk_b[1]).astype(jnp.float32)) / GLA_TAU
        gla = (heads(gq, GLA_HEADS) * GLA_DK ** -0.5, heads(gkk, GLA_HEADS), heads(gv, GLA_HEADS),
               gg, heads(dec_f, GLA_HEADS), heads(dec_b, GLA_HEADS))
        att = (rms_norm(heads(bq, B_HEADS), qn_g), rms_norm(heads(bk, B_KV_HEADS), kn_g), heads(bv, B_KV_HEADS))
        return gla, att

    (qx, kx, vx, gx, dfx, dbx), (aqx, akx, avx) = project(hx)
    (qc, kc, vc, gc, dfc, dbc), (aqc, akc, avc) = project(hc)

    def flip(t):
        return t[:, ::-1]

    s0 = jnp.zeros((hc.shape[0], GLA_HEADS, GLA_DK, GLA_DV), jnp.float32)
    ocf, s_f = gla_chunk_scan(qc, kc, vc, dfc, s0)
    ocb, s_b = gla_chunk_scan(flip(qc), flip(kc), flip(vc), flip(dbc), s0)
    oxf, _ = gla_chunk_scan(qx, kx, vx, dfx, s_f)
    oxb, _ = gla_chunk_scan(flip(qx), flip(kx), flip(vx), flip(dbx), s_b)

    def gla_out(o, gate):
        o = rms_norm(o.astype(gate.dtype), gla_g) * jax.nn.silu(heads(gate, GLA_HEADS))
        return o.reshape(o.shape[0], o.shape[1], GLA_V_W)

    aqx, akx = apply_rope(aqx, cos, sin), apply_rope(akx, cos, sin)
    k_all = jnp.concatenate([akx, akc], axis=1)
    v_all = jnp.concatenate([avx, avc], axis=1)
    ax = dense_block_attention(aqx, k_all, v_all)
    ox = jnp.concatenate([gla_out(oxf + flip(oxb), gx), ax], axis=-1) @ w_out
    if not with_ctx:
        return ox, None
    ac = dense_block_attention(aqc, akc, avc)
    oc = jnp.concatenate([gla_out(ocf + flip(ocb), gc), ac], axis=-1) @ w_out
    return ox, oc


def odd_mixer(hx, hc, w_in, w_out, sink, cos, sin, with_ctx):
    qx, kx, vx = split_cols(hx @ w_in, ODD_WIDTHS)
    qc, kc, vc = split_cols(hc @ w_in, ODD_WIDTHS)
    qx = apply_rope(heads(qx, C_HEADS), cos, sin)
    kx = apply_rope(heads(kx, C_KV_HEADS), cos, sin)
    vx = heads(vx, C_KV_HEADS)
    qc, kc, vc = heads(qc, C_HEADS), heads(kc, C_KV_HEADS), heads(vc, C_KV_HEADS)
    ox = window_sink_attention(qx, kx, vx, kc, vc, sink) @ w_out
    if not with_ctx:
        return ox, None
    oc = ctx_sink_attention(qc, kc, vc, sink) @ w_out
    return ox, oc


def swiglu(h, w1, w3, w2):
    return (jax.nn.silu(h @ w1) * (h @ w3)) @ w2


def setup_inputs(seed: int = 0) -> dict:
    key = jax.random.key(seed)
    ks = jax.random.split(key, 24)

    def nrm(k, shape, scale):
        return jax.random.normal(k, shape, jnp.float32) * scale

    def gain(k, shape):
        return 1.0 + nrm(k, shape, 0.02)

    D = D_MODEL
    return {
        "x": nrm(ks[0], (BATCH, SEQ, D), 1.0),
        "c": nrm(ks[1], (BATCH, D), 1.0),
        "ctx": nrm(ks[2], (BATCH, CTX_LEN, D), 1.0),
        "c_ctx": nrm(ks[3], (D,), 1.0),
        "ada_w": nrm(ks[4], (DEPTH, D, 6 * D), D ** -0.5),
        "ada_b": nrm(ks[5], (DEPTH, 6 * D), 0.02),
        "norm1_g": gain(ks[6], (DEPTH, D)),
        "norm2_g": gain(ks[7], (DEPTH, D)),
        "ffn_w1": nrm(ks[8], (DEPTH, D, FFN_HIDDEN), D ** -0.5),
        "ffn_w3": nrm(ks[9], (DEPTH, D, FFN_HIDDEN), D ** -0.5),
        "ffn_w2": nrm(ks[10], (DEPTH, FFN_HIDDEN, D), FFN_HIDDEN ** -0.5),
        "mixab_w_in": nrm(ks[11], (N_EVEN, D, EVEN_IN), D ** -0.5),
        "mixab_w_out": nrm(ks[12], (N_EVEN, EVEN_MIX, D), EVEN_MIX ** -0.5),
        "gla_gk_w": nrm(ks[13], (N_EVEN, 2, GLA_RANK, GLA_QK_W), GLA_RANK ** -0.5),
        "gla_gk_b": nrm(ks[14], (N_EVEN, 2, GLA_QK_W), 0.02),
        "gla_norm_g": gain(ks[15], (N_EVEN, GLA_DV)),
        "attn_qnorm_g": gain(ks[16], (N_EVEN, HEAD_DIM)),
        "attn_knorm_g": gain(ks[17], (N_EVEN, HEAD_DIM)),
        "win_w_in": nrm(ks[18], (N_ODD, D, ODD_IN), D ** -0.5),
        "win_w_out": nrm(ks[19], (N_ODD, ODD_MIX, D), ODD_MIX ** -0.5),
        "win_sink": nrm(ks[20], (N_ODD, C_HEADS), 0.5),
        "final_g": gain(ks[21], (D,)),
    }


def reference(x, c, ctx, c_ctx, ada_w, ada_b, norm1_g, norm2_g, ffn_w1, ffn_w3, ffn_w2,
              mixab_w_in, mixab_w_out, gla_gk_w, gla_gk_b, gla_norm_g, attn_qnorm_g, attn_knorm_g,
              win_w_in, win_w_out, win_sink, final_g):
    n_tokens = x.shape[1]
    cos, sin = rope_tables(n_tokens, x.dtype)
    sc = jax.nn.silu(c)
    scc = jax.nn.silu(c_ctx)[None]
    xc = ctx
    for l in range(DEPTH):
        with_ctx = l < DEPTH - 1
        mod_x = (sc @ ada_w[l] + ada_b[l])[:, None, :]
        mod_c = (scc @ ada_w[l] + ada_b[l])[:, None, :]
        sh1x, sc1x, g1x, sh2x, sc2x, g2x = jnp.split(mod_x, 6, axis=-1)
        sh1c, sc1c, g1c, sh2c, sc2c, g2c = jnp.split(mod_c, 6, axis=-1)
        hx = modulate(rms_norm(x, norm1_g[l]), sh1x, sc1x)
        hc = modulate(rms_norm(xc, norm1_g[l]), sh1c, sc1c)
        if l % 2 == 0:
            i = l // 2
            ox, oc = even_mixer(hx, hc, mixab_w_in[i], mixab_w_out[i], gla_gk_w[i], gla_gk_b[i],
                                gla_norm_g[i], attn_qnorm_g[i], attn_knorm_g[i], cos, sin, with_ctx)
        else:
            i = l // 2
            ox, oc = odd_mixer(hx, hc, win_w_in[i], win_w_out[i], win_sink[i], cos, sin, with_ctx)
        x = x + g1x * ox
        x = x + g2x * swiglu(modulate(rms_norm(x, norm2_g[l]), sh2x, sc2x), ffn_w1[l], ffn_w3[l], ffn_w2[l])
        if with_ctx:
            xc = xc + g1c * oc
            xc = xc + g2c * swiglu(modulate(rms_norm(xc, norm2_g[l]), sh2c, sc2c), ffn_w1[l], ffn_w3[l], ffn_w2[l])
    return rms_norm(x, final_g)
```

```python
import functools

import numpy as np
import jax
import jax.numpy as jnp
from jax import lax
from jax.experimental import pallas as pl
from jax.experimental.pallas import tpu as pltpu

F32 = jnp.float32
BF16 = jnp.bfloat16

D_MODEL = 1024
BATCH = 8
SEQ = 2048
DEPTH = 4
GRID_W = 64
CTX_LEN = 256
HEAD_DIM = 64
ROPE_THETA = 10000.0
EPS = 1e-6
WINDOW = 128
GLA_HEADS = 8
GLA_DK = 32
GLA_DV = 64
GLA_RANK = 16
GLA_TAU = 16.0
GLA_QK_W = GLA_HEADS * GLA_DK
GLA_V_W = GLA_HEADS * GLA_DV
B_HEADS = 8
C_HEADS = 16
FFN_HIDDEN = 2816

NX = BATCH * SEQ
NC = BATCH * CTX_LEN
NTOK = NX + NC

LANES = 128
ROW_TILE = 512
ATT_TQ = 128
GLA_CHUNK = 128
FFN_TH = FFN_HIDDEN // 2
VMEM_LIMIT = 52 * 1024 * 1024
NEG_BIG = -1e30

N_X_TILES = NX // ROW_TILE
N_ALL_TILES = NTOK // ROW_TILE
CTX_MOD_ROW = BATCH
MOD_ROWS = 16


def _cparams(*sem):
    return pltpu.CompilerParams(dimension_semantics=sem, vmem_limit_bytes=VMEM_LIMIT)


def _dot(a, b):
    return jnp.dot(a, b, preferred_element_type=F32)


def _dot_nt(a, b):
    return lax.dot_general(a, b, (((1,), (1,)), ((), ())), preferred_element_type=F32)


def _mod_row(i):
    return jnp.where(i < N_X_TILES, i // (SEQ // ROW_TILE), CTX_MOD_ROW)


def _mod_spec(layer, chunk, ngrid):
    if ngrid == 1:
        return pl.BlockSpec((None, None, 1, D_MODEL), lambda i: (layer, _mod_row(i), 0, chunk))
    return pl.BlockSpec((None, None, 1, D_MODEL), lambda i, k: (layer, _mod_row(i), 0, chunk))


def _full_spec(shape, ngrid=1):
    zeros = (0,) * len(shape)
    if ngrid == 1:
        return pl.BlockSpec(shape, lambda i: zeros)
    if ngrid == 2:
        return pl.BlockSpec(shape, lambda i, j: zeros)
    return pl.BlockSpec(shape, lambda i, j, k: zeros)


def _ada_kernel(s_ref, w_ref, b_ref, o_ref):
    s = s_ref[...]
    s = (s / (1.0 + jnp.exp(-s))).astype(BF16)
    o_ref[...] = _dot(s, w_ref[...].astype(BF16)) + b_ref[...]


def _ada_call(cvec, ada_w, ada_b):
    tn = 1536
    return pl.pallas_call(
        _ada_kernel,
        grid=(DEPTH, 6 * D_MODEL // tn),
        in_specs=[
            pl.BlockSpec((MOD_ROWS, D_MODEL), lambda l, j: (0, 0)),
            pl.BlockSpec((None, D_MODEL, tn), lambda l, j: (l, 0, j)),
            pl.BlockSpec((None, 1, tn), lambda l, j: (l, 0, j)),
        ],
        out_specs=pl.BlockSpec((None, MOD_ROWS, tn), lambda l, j: (l, 0, j)),
        out_shape=jax.ShapeDtypeStruct((DEPTH, MOD_ROWS, 6 * D_MODEL), F32),
        compiler_params=_cparams("parallel", "parallel"),
        name="ada_mod",
    )(cvec, ada_w, ada_b.reshape(DEPTH, 1, 6 * D_MODEL))


def _norm_mod(x, g_ref, shift_ref, scale_ref):
    ms = jnp.mean(x * x, axis=-1, keepdims=True)
    y = x * lax.rsqrt(ms + EPS) * g_ref[...]
    return (y * (1.0 + scale_ref[...]) + shift_ref[...]).astype(BF16)


def _rope(a, cos, sin_a, sin_b):
    return (a * cos + pltpu.roll(a, LANES - 16, 1) * sin_a + pltpu.roll(a, 16, 1) * sin_b)


def _head_rms(a, bd_ref, g_ref):
    ms = _dot((a * a).astype(BF16), bd_ref[...])
    return a * lax.rsqrt(ms + EPS) * g_ref[...]


def _even_in_kernel(x_ref, shift_ref, scale_ref, g_ref, w_ref, wlr_ref, wg_ref, gb_ref,
                    qn_ref, kn_ref, bd_ref, cos_ref, sa_ref, sb_ref,
                    qg_ref, kg_ref, vg_ref, gg_ref, dec_ref, bq_ref, bk_ref, bv_ref):
    h = _norm_mod(x_ref[...], g_ref, shift_ref, scale_ref)

    def proj(lo, hi):
        return _dot(h, w_ref[:, lo:hi])

    qg_ref[...] = (proj(0, 256) * GLA_DK ** -0.5).astype(BF16)
    kg_ref[...] = proj(256, 512).astype(BF16)
    vg_ref[...] = proj(512, 1024).astype(BF16)
    gg_ref[...] = proj(1024, 1536).astype(BF16)

    lr = _dot(h, wlr_ref[...]).astype(BF16)
    z = _dot(lr, wg_ref[...]) + gb_ref[...]
    logsig = jnp.minimum(z, 0.0) - jnp.log(1.0 + jnp.exp(-jnp.abs(z)))
    dec = logsig * (1.0 / GLA_TAU)
    dec_ref[0] = dec[:, :GLA_QK_W]
    dec_ref[1] = dec[:, GLA_QK_W:]

    cos, sa, sb = cos_ref[...], sa_ref[...], sb_ref[...]
    for p in range(4):
        a = _head_rms(proj(1536 + LANES * p, 1536 + LANES * (p + 1)), bd_ref, qn_ref)
        bq_ref[:, LANES * p:LANES * (p + 1)] = (_rope(a, cos, sa, sb) * HEAD_DIM ** -0.5).astype(BF16)
    a = _head_rms(proj(2048, 2176), bd_ref, kn_ref)
    bk_ref[...] = _rope(a, cos, sa, sb).astype(BF16)
    bv_ref[...] = proj(2176, 2304).astype(BF16)


def _odd_in_kernel(x_ref, shift_ref, scale_ref, g_ref, w_ref, cos_ref, sa_ref, sb_ref,
                   q_ref, k_ref, v_ref):
    h = _norm_mod(x_ref[...], g_ref, shift_ref, scale_ref)
    cos, sa, sb = cos_ref[...], sa_ref[...], sb_ref[...]
    for p in range(C_HEADS // 2):
        a = _dot(h, w_ref[:, LANES * p:LANES * (p + 1)])
        q_ref[:, LANES * p:LANES * (p + 1)] = (_rope(a, cos, sa, sb) * HEAD_DIM ** -0.5).astype(BF16)
    k_ref[...] = _rope(_dot(h, w_ref[:, 1024:1152]), cos, sa, sb).astype(BF16)
    v_ref[...] = _dot(h, w_ref[:, 1152:1280]).astype(BF16)


def _rope_spec():
    per = SEQ // ROW_TILE
    return pl.BlockSpec((ROW_TILE, LANES), lambda i: (jnp.where(i < N_X_TILES, i % per, per), 0))


def _row_spec(width):
    return pl.BlockSpec((ROW_TILE, width), lambda i: (i, 0))


def _even_in_call(xs, mod4, layer, norm_g, w_main, w_lr, w_gk, gk_bias, qn, kn, bd128, rope):
    outs = [(GLA_QK_W, BF16), (GLA_QK_W, BF16), (GLA_V_W, BF16), (GLA_V_W, BF16)]
    out_shape = [jax.ShapeDtypeStruct((NTOK, w), dt) for w, dt in outs]
    out_specs = [_row_spec(w) for w, _ in outs]
    out_shape.append(jax.ShapeDtypeStruct((2, NTOK, GLA_QK_W), F32))
    out_specs.append(pl.BlockSpec((2, ROW_TILE, GLA_QK_W), lambda i: (0, i, 0)))
    for w in (512, LANES, LANES):
        out_shape.append(jax.ShapeDtypeStruct((NTOK, w), BF16))
        out_specs.append(_row_spec(w))
    return pl.pallas_call(
        _even_in_kernel,
        grid=(N_ALL_TILES,),
        in_specs=[
            _row_spec(D_MODEL), _mod_spec(layer, 0, 1), _mod_spec(layer, 1, 1),
            _full_spec((1, D_MODEL)), _full_spec(w_main.shape), _full_spec(w_lr.shape),
            _full_spec(w_gk.shape), _full_spec(gk_bias.shape),
            _full_spec((1, LANES)), _full_spec((1, LANES)), _full_spec((LANES, LANES)),
            _rope_spec(), _rope_spec(), _rope_spec(),
        ],
        out_specs=out_specs,
        out_shape=out_shape,
        compiler_params=_cparams("parallel"),
        name="even_in_proj",
    )(xs, mod4, mod4, norm_g, w_main, w_lr, w_gk, gk_bias, qn, kn, bd128, *rope)


def _odd_in_call(xs, mod4, layer, norm_g, w_in, rope):
    widths = (C_HEADS * HEAD_DIM, LANES, LANES)
    return pl.pallas_call(
        _odd_in_kernel,
        grid=(N_ALL_TILES,),
        in_specs=[
            _row_spec(D_MODEL), _mod_spec(layer, 0, 1), _mod_spec(layer, 1, 1),
            _full_spec((1, D_MODEL)), _full_spec(w_in.shape),
            _rope_spec(), _rope_spec(), _rope_spec(),
        ],
        out_specs=[_row_spec(w) for w in widths],
        out_shape=[jax.ShapeDtypeStruct((NTOK, w), BF16) for w in widths],
        compiler_params=_cparams("parallel"),
        name="odd_in_proj",
    )(xs, mod4, mod4, norm_g, w_in, *rope)


def _attn_kernel(*refs, n_heads, window, has_sink, ctx_tiles):
    if has_sink:
        q_ref, kx_ref, vx_ref, kc_ref, vc_ref, sink_ref, o_ref, p_scr = refs
    else:
        q_ref, kx_ref, vx_ref, kc_ref, vc_ref, o_ref, p_scr = refs
        sink_ref = None
    tq = ATT_TQ
    nslab = n_heads // 2
    t = pl.program_id(1)
    low = lax.broadcasted_iota(jnp.int32, (tq, LANES), 1) < HEAD_DIM

    def stacked_q():
        q = q_ref[...]
        zero = jnp.zeros((tq, LANES), BF16)
        parts = []
        for kvh in range(2):
            keep = low if kvh == 0 else jnp.logical_not(low)
            for p in range(nslab):
                parts.append(jnp.where(keep, q[:, LANES * p:LANES * (p + 1)], zero))
        return jnp.concatenate(parts, axis=0)

    def softmax_pv(scores, values, biases):
        offs, o0 = [], 0
        for s in scores:
            offs.append((o0, o0 + s.shape[1]))
            o0 += s.shape[1]
        denoms = []
        for hh in range(n_heads):
            rows = slice(hh * tq, (hh + 1) * tq)
            ss = [s[rows] if b is None else s[rows] + b for s, b in zip(scores, biases)]
            m = functools.reduce(jnp.maximum, [jnp.max(x, axis=-1, keepdims=True) for x in ss])
            if has_sink:
                sk = sink_ref[hh]
                m = jnp.maximum(m, sk)
            den = jnp.zeros((tq, 1), F32)
            for x, (c0, c1) in zip(ss, offs):
                p = jnp.exp(x - m)
                den = den + jnp.sum(p, axis=-1, keepdims=True)
                p_scr[rows, c0:c1] = p.astype(BF16)
            if has_sink:
                den = den + jnp.exp(sk - m)
            denoms.append(den)
        acc = None
        for v, (c0, c1) in zip(values, offs):
            part = _dot(p_scr[:, c0:c1], v)
            acc = part if acc is None else acc + part
        for p in range(nslab):
            a0 = acc[p * tq:(p + 1) * tq] / denoms[p]
            a1 = acc[(nslab + p) * tq:(nslab + p + 1) * tq] / denoms[nslab + p]
            o_ref[:, LANES * p:LANES * (p + 1)] = jnp.where(low, a0, a1).astype(BF16)

    def latent_tile():
        qs = stacked_q()
        kc, vc = kc_ref[...], vc_ref[...]
        if window:
            band = tq + 2 * WINDOW
            start = pl.multiple_of(jnp.clip(t * tq - WINDOW, 0, SEQ - band), LANES)
            kb = kx_ref[pl.ds(start, band), :]
            vb = vx_ref[pl.ds(start, band), :]
            qpos = t * tq + lax.broadcasted_iota(jnp.int32, (tq, band), 0)
            kpos = start + lax.broadcasted_iota(jnp.int32, (tq, band), 1)
            bias = jnp.where(jnp.abs(qpos - kpos) <= WINDOW, 0.0, NEG_BIG).astype(F32)
            softmax_pv([_dot_nt(qs, kb), _dot_nt(qs, kc)], [vb, vc], [bias, None])
        else:
            softmax_pv([_dot_nt(qs, kx_ref[...]), _dot_nt(qs, kc)], [vx_ref[...], vc], [None, None])

    def context_tile():
        qs = stacked_q()
        softmax_pv([_dot_nt(qs, kc_ref[...])], [vc_ref[...]], [None])

    if ctx_tiles:
        pl.when(t < SEQ // tq)(latent_tile)
        pl.when(t >= SEQ // tq)(context_tile)
    else:
        latent_tile()


def _attn_call(q, k, v, sink, *, n_heads, window, ctx_tiles):
    tq = ATT_TQ
    n_xt = SEQ // tq
    n_ct = CTX_LEN // tq
    width = n_heads * HEAD_DIM
    has_sink = sink is not None
    n_keys = (tq + 2 * WINDOW if window else SEQ) + CTX_LEN

    def q_map(b, t):
        return (jnp.where(t < n_xt, b * n_xt + t, NX // tq + b * n_ct + (t - n_xt)), 0)

    x_spec = pl.BlockSpec((SEQ, LANES), lambda b, t: (b, 0))
    c_spec = pl.BlockSpec((CTX_LEN, LANES), lambda b, t: (NX // CTX_LEN + b, 0))
    in_specs = [pl.BlockSpec((tq, width), q_map), x_spec, x_spec, c_spec, c_spec]
    args = [q, k, v, k, v]
    if has_sink:
        in_specs.append(pl.BlockSpec(memory_space=pltpu.SMEM))
        args.append(sink)
    rows = NTOK if ctx_tiles else NX
    return pl.pallas_call(
        functools.partial(_attn_kernel, n_heads=n_heads, window=window, has_sink=has_sink,
                          ctx_tiles=ctx_tiles),
        grid=(BATCH, n_xt + (n_ct if ctx_tiles else 0)),
        in_specs=in_specs,
        out_specs=pl.BlockSpec((tq, width), q_map),
        out_shape=jax.ShapeDtypeStruct((rows, width), BF16),
        scratch_shapes=[pltpu.VMEM((n_heads * tq, n_keys), BF16)],
        compiler_params=_cparams("parallel", "arbitrary"),
        name="win_attn" if window else "dense_attn",
    )(*args)


GLA_N_CTX = CTX_LEN // GLA_CHUNK
GLA_N_X = SEQ // GLA_CHUNK
GLA_STEPS = GLA_N_CTX + GLA_N_X


def _gla_block(b, d, t):
    ctx_blk = NX // GLA_CHUNK + b * GLA_N_CTX + jnp.where(d == 0, t, GLA_N_CTX - 1 - t)
    x_blk = b * GLA_N_X + jnp.where(d == 0, t - GLA_N_CTX, GLA_STEPS - 1 - t)
    return jnp.where(t < GLA_N_CTX, ctx_blk, x_blk)


def _gla_kernel(q_ref, k_ref, v_ref, dec_ref, gate_ref, gn_ref, bmask_ref, hmask_ref, bd_ref,
                o_ref, s_scr, of_scr):
    C = GLA_CHUNK
    d = pl.program_id(1)
    t = pl.program_id(2)
    fwd = d == 0

    @pl.when(t == 0)
    def _():
        s_scr[...] = jnp.zeros_like(s_scr)

    g = dec_ref[...]
    row = lax.broadcasted_iota(jnp.int32, (C, C), 0)
    col = lax.broadcasted_iota(jnp.int32, (C, C), 1)
    sgn = 1 - 2 * d
    tri = ((row - col) * sgn >= 0).astype(F32).astype(BF16)
    g_hi = g.astype(BF16)
    r1 = g - g_hi.astype(F32)
    g_mid = r1.astype(BF16)
    g_lo = (r1 - g_mid.astype(F32)).astype(BF16)
    bc = _dot(tri, g_hi) + _dot(tri, g_mid) + _dot(tri, g_lo)
    tot = jnp.where(fwd, bc[C - 1:C, :], bc[0:1, :])

    qf = q_ref[...].astype(F32)
    kf = k_ref[...].astype(F32)
    v = v_ref[...]
    qb = (qf * jnp.exp(bc)).astype(BF16)
    kp = (kf * jnp.exp(-bc)).astype(BF16)
    s_old = s_scr[...]
    o = _dot(qb, s_old.astype(BF16))

    qst = jnp.concatenate([qb * hmask_ref[h] for h in range(GLA_HEADS)], axis=0)
    s = _dot_nt(qst, kp)
    r8 = lax.broadcasted_iota(jnp.int32, (GLA_HEADS * C, C), 0) & (C - 1)
    c8 = lax.broadcasted_iota(jnp.int32, (GLA_HEADS * C, C), 1)
    a = jnp.where((r8 - c8) * sgn >= 0, s, 0.0).astype(BF16)
    low = lax.broadcasted_iota(jnp.int32, (C, LANES), 1) < GLA_DV
    parts = []
    for p in range(GLA_HEADS // 2):
        r = _dot(a[2 * p * C:(2 * p + 2) * C], v[:, LANES * p:LANES * (p + 1)])
        parts.append(jnp.where(low, r[:C], r[C:]))
    o = o + jnp.concatenate(parts, axis=1)

    kt = (kf * jnp.exp(tot - bc)).T.astype(BF16)
    kv = _dot(kt, v)
    dcol = jnp.exp(jnp.broadcast_to(tot, (LANES, GLA_QK_W))).T
    dmat = jnp.concatenate([dcol] * (GLA_V_W // LANES), axis=1)
    s_scr[...] = dmat * s_old + bmask_ref[...] * kv

    slot = jnp.where(fwd, t, jnp.where(t < GLA_N_CTX, GLA_N_CTX - 1 - t,
                                       GLA_STEPS + GLA_N_CTX - 1 - t))

    @pl.when(fwd)
    def _():
        of_scr[slot] = o

    @pl.when(jnp.logical_not(fwd))
    def _():
        ot = of_scr[slot] + o
        ms = _dot((ot * ot).astype(BF16), bd_ref[...])
        y = ot * lax.rsqrt(ms + EPS) * gn_ref[...]
        gt = gate_ref[...].astype(F32)
        o_ref[...] = (y * (gt / (1.0 + jnp.exp(-gt)))).astype(BF16)


def _gla_call(qg, kg, vg, gg, dec, gn512, bmask, hmask, bd512):
    C = GLA_CHUNK

    def blk(b, d, t):
        return (_gla_block(b, d, t), 0)

    def out_blk(b, d, t):
        return (_gla_block(b, 1, jnp.where(d == 0, 0, t)), 0)

    return pl.pallas_call(
        _gla_kernel,
        grid=(BATCH, 2, GLA_STEPS),
        in_specs=[
            pl.BlockSpec((C, GLA_QK_W), blk), pl.BlockSpec((C, GLA_QK_W), blk),
            pl.BlockSpec((C, GLA_V_W), blk),
            pl.BlockSpec((None, C, GLA_QK_W), lambda b, d, t: (d, _gla_block(b, d, t), 0)),
            pl.BlockSpec((C, GLA_V_W), out_blk),
            _full_spec((1, GLA_V_W), 3), _full_spec(bmask.shape, 3), _full_spec(hmask.shape, 3),
            _full_spec(bd512.shape, 3),
        ],
        out_specs=pl.BlockSpec((C, GLA_V_W), out_blk),
        out_shape=jax.ShapeDtypeStruct((NTOK, GLA_V_W), BF16),
        scratch_shapes=[pltpu.VMEM((GLA_QK_W, GLA_V_W), F32),
                        pltpu.VMEM((GLA_STEPS, C, GLA_V_W), F32)],
        compiler_params=_cparams("parallel", "arbitrary", "arbitrary"),
        name="gla_scan",
    )(qg, kg, vg, dec, gg, gn512, bmask, hmask, bd512)


def _out_kernel(ma_ref, mb_ref, wa_ref, wb_ref, x_ref, g1_ref, shift_ref, scale_ref, n2_ref,
                x1_ref, h2_ref):
    ox = _dot(ma_ref[...], wa_ref[...]) + _dot(mb_ref[...], wb_ref[...])
    x1 = x_ref[...] + g1_ref[...] * ox
    x1_ref[...] = x1
    h2_ref[...] = _norm_mod(x1, n2_ref, shift_ref, scale_ref)


def _out_call(mix_a, mix_b, col_b, w_a, w_b, xs, mod4, layer, norm_g, n_tiles):
    half = D_MODEL // 2
    rows = n_tiles * ROW_TILE
    return pl.pallas_call(
        _out_kernel,
        grid=(n_tiles,),
        in_specs=[
            pl.BlockSpec((ROW_TILE, half), lambda i: (i, 0)),
            pl.BlockSpec((ROW_TILE, half), lambda i: (i, col_b)),
            _full_spec((half, D_MODEL)), _full_spec((half, D_MODEL)),
            _row_spec(D_MODEL),
            _mod_spec(layer, 2, 1), _mod_spec(layer, 3, 1), _mod_spec(layer, 4, 1),
            _full_spec((1, D_MODEL)),
        ],
        out_specs=[_row_spec(D_MODEL), _row_spec(D_MODEL)],
        out_shape=[jax.ShapeDtypeStruct((rows, D_MODEL), F32),
                   jax.ShapeDtypeStruct((rows, D_MODEL), BF16)],
        compiler_params=_cparams("parallel"),
        name="out_proj",
    )(mix_a, mix_b, w_a, w_b, xs, mod4, mod4, mod4, norm_g)


def _ffn_kernel(h_ref, x1_ref, w1_ref, w3_ref, w2_ref, g2_ref, fg_ref, o_ref, acc_ref, *, final):
    k = pl.program_id(1)
    h = h_ref[...]
    a = _dot(h, w1_ref[...])
    u = ((a / (1.0 + jnp.exp(-a))) * _dot(h, w3_ref[...])).astype(BF16)
    part = _dot(u, w2_ref[...])

    @pl.when(k == 0)
    def _():
        acc_ref[...] = part

    @pl.when(k > 0)
    def _():
        acc_ref[...] += part

    @pl.when(k == pl.num_programs(1) - 1)
    def _():
        x2 = x1_ref[...] + g2_ref[...] * acc_ref[...]
        if final:
            ms = jnp.mean(x2 * x2, axis=-1, keepdims=True)
            x2 = x2 * lax.rsqrt(ms + EPS) * fg_ref[...]
        o_ref[...] = x2


def _ffn_call(h2, x1, w1, w3, w2, mod4, layer, final_g, final):
    rows = h2.shape[0]
    th = FFN_TH
    return pl.pallas_call(
        functools.partial(_ffn_kernel, final=final),
        grid=(rows // ROW_TILE, FFN_HIDDEN // th),
        in_specs=[
            pl.BlockSpec((ROW_TILE, D_MODEL), lambda i, k: (i, 0)),
            pl.BlockSpec((ROW_TILE, D_MODEL), lambda i, k: (i, 0)),
            pl.BlockSpec((D_MODEL, th), lambda i, k: (0, k)),
            pl.BlockSpec((D_MODEL, th), lambda i, k: (0, k)),
            pl.BlockSpec((th, D_MODEL), lambda i, k: (k, 0)),
            _mod_spec(layer, 5, 2),
            _full_spec((1, D_MODEL), 2),
        ],
        out_specs=pl.BlockSpec((ROW_TILE, D_MODEL), lambda i, k: (i, 0)),
        out_shape=jax.ShapeDtypeStruct((rows, D_MODEL), F32),
        scratch_shapes=[pltpu.VMEM((ROW_TILE, D_MODEL), F32)],
        compiler_params=_cparams("parallel", "arbitrary"),
        name="ffn",
    )(h2, x1, w1, w3, w2, mod4, final_g)


def _slab_head_perm(n_heads):
    half = n_heads // 2
    cols = []
    for p in range(half):
        cols.append(np.arange(HEAD_DIM) + HEAD_DIM * p)
        cols.append(np.arange(HEAD_DIM) + HEAD_DIM * (half + p))
    return np.concatenate(cols)


def _rope_tables():
    pos = np.arange(SEQ)
    half = HEAD_DIM // 2
    inv = ROPE_THETA ** (-jnp.arange(0, half, 2, dtype=F32) / half)
    ang_r = (pos // GRID_W).astype(np.float32)[:, None] * inv[None]
    ang_c = (pos % GRID_W).astype(np.float32)[:, None] * inv[None]
    ang = jnp.concatenate([ang_r, ang_r, ang_c, ang_c] * 2, axis=-1)
    first = (np.arange(LANES) % 32) < 16
    cos, sin = jnp.cos(ang), jnp.sin(ang)
    sin_a = jnp.where(first[None], -sin, 0.0)
    sin_b = jnp.where(first[None], 0.0, sin)
    ident = jnp.ones((ROW_TILE, LANES), F32)
    zero = jnp.zeros((ROW_TILE, LANES), F32)
    return (jnp.concatenate([cos, ident]), jnp.concatenate([sin_a, zero]),
            jnp.concatenate([sin_b, zero]))


def _block_mean_matrix(width):
    idx = np.arange(width) // HEAD_DIM
    return jnp.asarray((idx[:, None] == idx[None, :]).astype(np.float32) / HEAD_DIM, BF16)


def kernel(x, c, ctx, c_ctx, ada_w, ada_b, norm1_g, norm2_g, ffn_w1, ffn_w3, ffn_w2,
           mixab_w_in, mixab_w_out, gla_gk_w, gla_gk_b, gla_norm_g, attn_qnorm_g, attn_knorm_g,
           win_w_in, win_w_out, win_sink, final_g):
    xs = jnp.concatenate([x.reshape(NX, D_MODEL), ctx.reshape(NC, D_MODEL)], axis=0)
    cvec = jnp.concatenate([c, c_ctx[None], jnp.zeros((MOD_ROWS - BATCH - 1, D_MODEL), F32)], axis=0)
    mod4 = _ada_call(cvec, ada_w, ada_b).reshape(DEPTH, MOD_ROWS, 1, 6 * D_MODEL)

    rope = _rope_tables()
    bd128 = _block_mean_matrix(LANES)
    bd512 = _block_mean_matrix(GLA_V_W)
    head_of_k = np.arange(GLA_QK_W) // GLA_DK
    head_of_v = np.arange(GLA_V_W) // GLA_DV
    bmask = jnp.asarray((head_of_k[:, None] == head_of_v[None, :]).astype(np.float32))
    hmask = jnp.asarray((np.arange(GLA_HEADS)[:, None] == head_of_k[None, :]).astype(np.float32),
                        BF16).reshape(GLA_HEADS, 1, GLA_QK_W)
    perm8 = _slab_head_perm(B_HEADS)
    perm16 = _slab_head_perm(C_HEADS)
    final_row = final_g.reshape(1, D_MODEL)

    for l in range(DEPTH):
        with_ctx = l < DEPTH - 1
        i = l // 2
        n1 = norm1_g[l].reshape(1, D_MODEL)
        n2 = norm2_g[l].reshape(1, D_MODEL)
        if l % 2 == 0:
            w = mixab_w_in[i]
            w_main = jnp.concatenate([w[:, :1536], w[:, 1568:2080][:, perm8], w[:, 2080:]],
                                     axis=1).astype(BF16)
            w_lr = jnp.pad(w[:, 1536:1568], ((0, 0), (0, LANES - 2 * GLA_RANK))).astype(BF16)
            w_gk = jnp.zeros((LANES, 2 * GLA_QK_W), F32)
            w_gk = w_gk.at[:GLA_RANK, :GLA_QK_W].set(gla_gk_w[i, 0])
            w_gk = w_gk.at[GLA_RANK:2 * GLA_RANK, GLA_QK_W:].set(gla_gk_w[i, 1]).astype(BF16)
            gk_bias = gla_gk_b[i].reshape(1, 2 * GLA_QK_W)
            qn = jnp.tile(attn_qnorm_g[i], 2).reshape(1, LANES)
            kn = jnp.tile(attn_knorm_g[i], 2).reshape(1, LANES)
            qg, kg, vg, gg, dec, bq, bk, bv = _even_in_call(
                xs, mod4, l, n1, w_main, w_lr, w_gk, gk_bias, qn, kn, bd128, rope)
            gn512 = jnp.tile(gla_norm_g[i], GLA_HEADS).reshape(1, GLA_V_W)
            mix_a = _gla_call(qg, kg, vg, gg, dec, gn512, bmask, hmask, bd512)
            mix_b = _attn_call(bq, bk, bv, None, n_heads=B_HEADS, window=False, ctx_tiles=True)
            w_out = mixab_w_out[i]
            w_a = w_out[:GLA_V_W].astype(BF16)
            w_b = w_out[GLA_V_W:][perm8].astype(BF16)
            col_b = 0
        else:
            w = win_w_in[i]
            w_in = jnp.concatenate([w[:, :1024][:, perm16], w[:, 1024:]], axis=1).astype(BF16)
            q, k, v = _odd_in_call(xs, mod4, l, n1, w_in, rope)
            mix_a = _attn_call(q, k, v, win_sink[i], n_heads=C_HEADS, window=True,
                               ctx_tiles=with_ctx)
            mix_b = mix_a
            w_out = win_w_out[i][perm16].astype(BF16)
            w_a, w_b = w_out[:D_MODEL // 2], w_out[D_MODEL // 2:]
            col_b = 1
        n_tiles = N_ALL_TILES if with_ctx else N_X_TILES
        x1, h2 = _out_call(mix_a, mix_b, col_b, w_a, w_b, xs, mod4, l, n2, n_tiles)
        xs = _ffn_call(h2, x1, ffn_w1[l].astype(BF16), ffn_w3[l].astype(BF16),
                       ffn_w2[l].astype(BF16), mod4, l, final_row, final=not with_ctx)
    return xs.reshape(BATCH, SEQ, D_MODEL)
```

```python
import functools

import numpy as np
import jax
import jax.numpy as jnp
from jax import lax
from jax.experimental import pallas as pl
from jax.experimental.pallas import tpu as pltpu

F32 = jnp.float32
BF16 = jnp.bfloat16

D_MODEL = 1024
BATCH = 8
SEQ = 2048
DEPTH = 4
GRID_W = 64
CTX_LEN = 256
HEAD_DIM = 64
ROPE_THETA = 10000.0
EPS = 1e-6
WINDOW = 128
GLA_HEADS = 8
GLA_DK = 32
GLA_DV = 64
GLA_RANK = 16
GLA_TAU = 16.0
GLA_QK_W = GLA_HEADS * GLA_DK
GLA_V_W = GLA_HEADS * GLA_DV
B_HEADS = 8
C_HEADS = 16
FFN_HIDDEN = 2816

NX = BATCH * SEQ
NC = BATCH * CTX_LEN
NTOK = NX + NC

LANES = 128
ROW_TILE = 512
IN_SUBTILES = 2
IN_SUB_ROWS = ROW_TILE // IN_SUBTILES
ATT_TQ = 256
GLA_CHUNK = 256
MLP_HIDDEN_CHUNKS = 11
VMEM_LIMIT = 52 * 1024 * 1024
NEG_BIG = -1e30
LOG2E = 1.4426950408889634
Q_SCALE = HEAD_DIM ** -0.5 * LOG2E

N_X_TILES = NX // ROW_TILE
N_ALL_TILES = NTOK // ROW_TILE
CTX_MOD_ROW = BATCH
MOD_ROWS = 16


def _cparams(*sem):
    return pltpu.CompilerParams(dimension_semantics=sem, vmem_limit_bytes=VMEM_LIMIT)


def _dot(a, b):
    return jnp.dot(a, b, preferred_element_type=F32)


def _dot_nt(a, b):
    return lax.dot_general(a, b, (((1,), (1,)), ((), ())), preferred_element_type=F32)


def _mod_row(i):
    return jnp.where(i < N_X_TILES, i // (SEQ // ROW_TILE), CTX_MOD_ROW)


def _mod_spec(layer, chunk):
    return pl.BlockSpec((None, None, 1, D_MODEL), lambda i: (layer, _mod_row(i), 0, chunk))


def _full_spec(shape, ngrid=1):
    zeros = (0,) * len(shape)
    if ngrid == 1:
        return pl.BlockSpec(shape, lambda i: zeros)
    return pl.BlockSpec(shape, lambda i, j, k: zeros)


def _ada_kernel(s_ref, w_ref, b_ref, o_ref):
    s = s_ref[...]
    s = (s / (1.0 + jnp.exp(-s))).astype(BF16)
    o_ref[...] = _dot(s, w_ref[...].astype(BF16)) + b_ref[...]


def _ada_call(cvec, ada_w, ada_b):
    tn = 1536
    return pl.pallas_call(
        _ada_kernel,
        grid=(DEPTH, 6 * D_MODEL // tn),
        in_specs=[
            pl.BlockSpec((MOD_ROWS, D_MODEL), lambda l, j: (0, 0)),
            pl.BlockSpec((None, D_MODEL, tn), lambda l, j: (l, 0, j)),
            pl.BlockSpec((None, 1, tn), lambda l, j: (l, 0, j)),
        ],
        out_specs=pl.BlockSpec((None, MOD_ROWS, tn), lambda l, j: (l, 0, j)),
        out_shape=jax.ShapeDtypeStruct((DEPTH, MOD_ROWS, 6 * D_MODEL), F32),
        compiler_params=_cparams("parallel", "parallel"),
        name="ada_mod",
    )(cvec, ada_w, ada_b.reshape(DEPTH, 1, 6 * D_MODEL))


def _norm_mod(x, g_ref, shift_ref, scale_ref):
    ms = jnp.mean(x * x, axis=-1, keepdims=True)
    y = x * lax.rsqrt(ms + EPS) * g_ref[...]
    return (y * (1.0 + scale_ref[...]) + shift_ref[...]).astype(BF16)


def _rope(a, cos, sin_a, sin_b):
    return (a * cos + pltpu.roll(a, LANES - 16, 1) * sin_a + pltpu.roll(a, 16, 1) * sin_b)


def _even_in_kernel(x_ref, shift_ref, scale_ref, g_ref, w_ref, wg_ref, gb_ref,
                    qn_ref, kn_ref, bd_ref, cos_ref, sa_ref, sb_ref,
                    qg_ref, kg_ref, vg_ref, gg_ref, dec_ref, bq_ref, bk_ref, bv_ref):
    for r in range(IN_SUBTILES):
        rows = pl.ds(r * IN_SUB_ROWS, IN_SUB_ROWS)
        h = _norm_mod(x_ref[rows, :], g_ref, shift_ref, scale_ref)

        def proj(lo, hi):
            return _dot(h, w_ref[:, lo:hi])

        qg_ref[rows, :] = (proj(0, 256) * GLA_DK ** -0.5).astype(BF16)
        kg_ref[rows, :] = proj(256, 512).astype(BF16)
        vg_ref[rows, :] = proj(512, 1024).astype(BF16)
        gg_ref[rows, :] = proj(1024, 1536).astype(BF16)

        a = proj(1536, 2432)
        bv_ref[rows, :LANES] = a[:, 640:768].astype(BF16)
        bv_ref[rows, LANES:] = jnp.ones((IN_SUB_ROWS, LANES), BF16)

        lr = a[:, 768:896].astype(BF16)
        z = _dot(lr, wg_ref[...]) + gb_ref[...]
        logsig = jnp.minimum(z, 0.0) - jnp.log(1.0 + jnp.exp(-jnp.abs(z)))
        dec = logsig * (1.0 / GLA_TAU)
        dec_ref[0, rows, :] = dec[:, :GLA_QK_W]
        dec_ref[1, rows, :] = dec[:, GLA_QK_W:]

        sq = (a[:, :640] * a[:, :640]).astype(BF16)
        bd = bd_ref[...]
        ms = [_dot(sq[:, 0:256], bd), _dot(sq[:, 256:512], bd),
              _dot(sq[:, 512:640], bd[:LANES, :LANES])]
        cos, sa, sb = cos_ref[rows, :], sa_ref[rows, :], sb_ref[rows, :]
        for p in range(5):
            g_row = qn_ref[...] if p < 4 else kn_ref[...]
            msp = ms[p // 2][:, LANES * (p % 2):LANES * (p % 2 + 1)]
            n = a[:, LANES * p:LANES * (p + 1)] * lax.rsqrt(msp + EPS) * g_row
            rp = _rope(n, cos, sa, sb)
            if p < 4:
                bq_ref[rows, LANES * p:LANES * (p + 1)] = (rp * Q_SCALE).astype(BF16)
            else:
                bk_ref[rows, :] = rp.astype(BF16)


def _odd_in_kernel(x_ref, shift_ref, scale_ref, g_ref, w_ref, cos_ref, sa_ref, sb_ref,
                   q_ref, k_ref, v_ref):
    for r in range(IN_SUBTILES):
        rows = pl.ds(r * IN_SUB_ROWS, IN_SUB_ROWS)
        h = _norm_mod(x_ref[rows, :], g_ref, shift_ref, scale_ref)
        cos, sa, sb = cos_ref[rows, :], sa_ref[rows, :], sb_ref[rows, :]
        a = _dot(h, w_ref[...])
        for p in range(C_HEADS // 2):
            rp = _rope(a[:, LANES * p:LANES * (p + 1)], cos, sa, sb)
            q_ref[rows, LANES * p:LANES * (p + 1)] = (rp * Q_SCALE).astype(BF16)
        k_ref[rows, :] = _rope(a[:, 1024:1152], cos, sa, sb).astype(BF16)
        v_ref[rows, :LANES] = a[:, 1152:1280].astype(BF16)
        v_ref[rows, LANES:] = jnp.ones((IN_SUB_ROWS, LANES), BF16)


def _rope_spec():
    per = SEQ // ROW_TILE
    return pl.BlockSpec((ROW_TILE, LANES), lambda i: (jnp.where(i < N_X_TILES, i % per, per), 0))


def _row_spec(width):
    return pl.BlockSpec((ROW_TILE, width), lambda i: (i, 0))


def _even_in_call(xs, mod4, layer, norm_g, w_main, w_gk, gk_bias, qn, kn, bd256, rope):
    outs = [(GLA_QK_W, BF16), (GLA_QK_W, BF16), (GLA_V_W, BF16), (GLA_V_W, BF16)]
    out_shape = [jax.ShapeDtypeStruct((NTOK, w), dt) for w, dt in outs]
    out_specs = [_row_spec(w) for w, _ in outs]
    out_shape.append(jax.ShapeDtypeStruct((2, NTOK, GLA_QK_W), F32))
    out_specs.append(pl.BlockSpec((2, ROW_TILE, GLA_QK_W), lambda i: (0, i, 0)))
    for w in (512, LANES, 2 * LANES):
        out_shape.append(jax.ShapeDtypeStruct((NTOK, w), BF16))
        out_specs.append(_row_spec(w))
    return pl.pallas_call(
        _even_in_kernel,
        grid=(N_ALL_TILES,),
        in_specs=[
            _row_spec(D_MODEL), _mod_spec(layer, 0), _mod_spec(layer, 1),
            _full_spec((1, D_MODEL)), _full_spec(w_main.shape),
            _full_spec(w_gk.shape), _full_spec(gk_bias.shape),
            _full_spec((1, LANES)), _full_spec((1, LANES)), _full_spec((2 * LANES, 2 * LANES)),
            _rope_spec(), _rope_spec(), _rope_spec(),
        ],
        out_specs=out_specs,
        out_shape=out_shape,
        compiler_params=_cparams("parallel"),
        name="even_in_proj",
    )(xs, mod4, mod4, norm_g, w_main, w_gk, gk_bias, qn, kn, bd256, *rope)


def _odd_in_call(xs, mod4, layer, norm_g, w_in, rope):
    widths = (C_HEADS * HEAD_DIM, LANES, 2 * LANES)
    return pl.pallas_call(
        _odd_in_kernel,
        grid=(N_ALL_TILES,),
        in_specs=[
            _row_spec(D_MODEL), _mod_spec(layer, 0), _mod_spec(layer, 1),
            _full_spec((1, D_MODEL)), _full_spec(w_in.shape),
            _rope_spec(), _rope_spec(), _rope_spec(),
        ],
        out_specs=[_row_spec(w) for w in widths],
        out_shape=[jax.ShapeDtypeStruct((NTOK, w), BF16) for w in widths],
        compiler_params=_cparams("parallel"),
        name="odd_in_proj",
    )(xs, mod4, mod4, norm_g, w_in, *rope)


def _attn_kernel(*refs, n_heads, window, has_sink, ctx_tiles):
    if has_sink:
        q_ref, kx_ref, vx_ref, kc_ref, vc_ref, sink_ref, o_ref = refs
    else:
        q_ref, kx_ref, vx_ref, kc_ref, vc_ref, o_ref = refs
        sink_ref = None
    tq = ATT_TQ
    nslab = n_heads // 2
    t = pl.program_id(1)
    low = lax.broadcasted_iota(jnp.int32, (tq, LANES), 1) < HEAD_DIM
    upper_rows = lax.broadcasted_iota(jnp.int32, (2 * tq, 1), 0) >= tq
    zero = jnp.zeros((tq, LANES), BF16)

    def slab(p, keys, values, biases):
        q = q_ref[:, LANES * p:LANES * (p + 1)]
        qs = jnp.concatenate([jnp.where(low, q, zero), jnp.where(low, zero, q)], axis=0)
        ss = [_dot_nt(qs, k) if b is None else _dot_nt(qs, k) + b for k, b in zip(keys, biases)]
        m = functools.reduce(jnp.maximum, [jnp.max(x, axis=-1, keepdims=True) for x in ss])
        if has_sink:
            sk = jnp.where(upper_rows, sink_ref[nslab + p], sink_ref[p]) * LOG2E
            m = jnp.maximum(m, sk)
        acc = None
        for x, v in zip(ss, values):
            part = _dot(jnp.exp2(x - m).astype(BF16), v)
            acc = part if acc is None else acc + part
        den = acc[:, LANES:]
        if has_sink:
            den = den + jnp.exp2(sk - m)
        o = acc[:, :LANES] / den
        o_ref[:, LANES * p:LANES * (p + 1)] = jnp.where(low, o[:tq], o[tq:]).astype(BF16)

    def latent_tile():
        kc, vc = kc_ref[...], vc_ref[...]
        if window:
            band = tq + 2 * WINDOW
            start = pl.multiple_of(jnp.clip(t * tq - WINDOW, 0, SEQ - band), LANES)
            kb = kx_ref[pl.ds(start, band), :]
            vb = vx_ref[pl.ds(start, band), :]
            qpos = t * tq + (lax.broadcasted_iota(jnp.int32, (2 * tq, band), 0) & (tq - 1))
            kpos = start + lax.broadcasted_iota(jnp.int32, (2 * tq, band), 1)
            bias = jnp.where(jnp.abs(qpos - kpos) <= WINDOW, 0.0, NEG_BIG).astype(F32)
            for p in range(nslab):
                slab(p, [kb, kc], [vb, vc], [bias, None])
        else:
            kx, vx = kx_ref[...], vx_ref[...]
            for p in range(nslab):
                slab(p, [kx, kc], [vx, vc], [None, None])

    def context_tile():
        kc, vc = kc_ref[...], vc_ref[...]
        for p in range(nslab):
            slab(p, [kc], [vc], [None])

    if ctx_tiles:
        pl.when(t < SEQ // tq)(latent_tile)
        pl.when(t >= SEQ // tq)(context_tile)
    else:
        latent_tile()


def _attn_call(q, k, v, sink, *, n_heads, window, ctx_tiles):
    tq = ATT_TQ
    n_xt = SEQ // tq
    n_ct = CTX_LEN // tq
    width = n_heads * HEAD_DIM
    has_sink = sink is not None

    def q_map(b, t):
        return (jnp.where(t < n_xt, b * n_xt + t, NX // tq + b * n_ct + (t - n_xt)), 0)

    in_specs = [
        pl.BlockSpec((tq, width), q_map),
        pl.BlockSpec((SEQ, LANES), lambda b, t: (b, 0)),
        pl.BlockSpec((SEQ, 2 * LANES), lambda b, t: (b, 0)),
        pl.BlockSpec((CTX_LEN, LANES), lambda b, t: (NX // CTX_LEN + b, 0)),
        pl.BlockSpec((CTX_LEN, 2 * LANES), lambda b, t: (NX // CTX_LEN + b, 0)),
    ]
    args = [q, k, v, k, v]
    if has_sink:
        in_specs.append(pl.BlockSpec(memory_space=pltpu.SMEM))
        args.append(sink)
    rows = NTOK if ctx_tiles else NX
    return pl.pallas_call(
        functools.partial(_attn_kernel, n_heads=n_heads, window=window, has_sink=has_sink,
                          ctx_tiles=ctx_tiles),
        grid=(BATCH, n_xt + (n_ct if ctx_tiles else 0)),
        in_specs=in_specs,
        out_specs=pl.BlockSpec((tq, width), q_map),
        out_shape=jax.ShapeDtypeStruct((rows, width), BF16),
        compiler_params=_cparams("parallel", "arbitrary"),
        name="win_attn" if window else "dense_attn",
    )(*args)


GLA_N_CTX = CTX_LEN // GLA_CHUNK
GLA_N_X = SEQ // GLA_CHUNK
GLA_STEPS = GLA_N_CTX + GLA_N_X


def _gla_block(b, d, t):
    ctx_blk = NX // GLA_CHUNK + b * GLA_N_CTX + jnp.where(d == 0, t, GLA_N_CTX - 1 - t)
    x_blk = b * GLA_N_X + jnp.where(d == 0, t - GLA_N_CTX, GLA_STEPS - 1 - t)
    return jnp.where(t < GLA_N_CTX, ctx_blk, x_blk)


def _gla_kernel(q_ref, k_ref, v_ref, dec_ref, gate_ref, gn_ref, bmask_ref, hmask_ref, bd_ref,
                o_ref, s_scr, of_scr):
    C = GLA_CHUNK
    d = pl.program_id(1)
    t = pl.program_id(2)
    fwd = d == 0

    @pl.when(t == 0)
    def _():
        s_scr[...] = jnp.zeros_like(s_scr)

    g = dec_ref[...]
    row = lax.broadcasted_iota(jnp.int32, (C, C), 0)
    col = lax.broadcasted_iota(jnp.int32, (C, C), 1)
    sgn = 1 - 2 * d
    tri = ((row - col) * sgn >= 0).astype(F32).astype(BF16)
    g_hi = g.astype(BF16)
    r1 = g - g_hi.astype(F32)
    g_mid = r1.astype(BF16)
    g_lo = (r1 - g_mid.astype(F32)).astype(BF16)
    bc = _dot(tri, g_hi) + _dot(tri, g_mid) + _dot(tri, g_lo)
    tot = jnp.where(fwd, bc[C - 1:C, :], bc[0:1, :])

    qf = q_ref[...].astype(F32)
    kf = k_ref[...].astype(F32)
    v = v_ref[...]
    qb = (qf * jnp.exp(bc)).astype(BF16)
    kp = (kf * jnp.exp(-bc)).astype(BF16)
    s_old = s_scr[...]
    o = _dot(qb, s_old.astype(BF16))

    qst = jnp.concatenate([qb * hmask_ref[h] for h in range(GLA_HEADS)], axis=0)
    s = _dot_nt(qst, kp)
    r8 = lax.broadcasted_iota(jnp.int32, (GLA_HEADS * C, C), 0) & (C - 1)
    c8 = lax.broadcasted_iota(jnp.int32, (GLA_HEADS * C, C), 1)
    a = jnp.where((r8 - c8) * sgn >= 0, s, 0.0).astype(BF16)
    low = lax.broadcasted_iota(jnp.int32, (C, LANES), 1) < GLA_DV
    parts = []
    for p in range(GLA_HEADS // 2):
        r = _dot(a[2 * p * C:(2 * p + 2) * C], v[:, LANES * p:LANES * (p + 1)])
        parts.append(jnp.where(low, r[:C], r[C:]))
    o = o + jnp.concatenate(parts, axis=1)

    kt = (kf * jnp.exp(tot - bc)).T.astype(BF16)
    kv = _dot(kt, v)
    dcol = jnp.exp(jnp.broadcast_to(tot, (LANES, GLA_QK_W))).T
    dmat = jnp.concatenate([dcol] * (GLA_V_W // LANES), axis=1)
    s_scr[...] = dmat * s_old + bmask_ref[...] * kv

    slot = jnp.where(fwd, t, jnp.where(t < GLA_N_CTX, GLA_N_CTX - 1 - t,
                                       GLA_STEPS + GLA_N_CTX - 1 - t))

    @pl.when(fwd)
    def _():
        of_scr[slot] = o

    @pl.when(jnp.logical_not(fwd))
    def _():
        ot = of_scr[slot] + o
        ms = _dot((ot * ot).astype(BF16), bd_ref[...])
        y = ot * lax.rsqrt(ms + EPS) * gn_ref[...]
        gt = gate_ref[...].astype(F32)
        o_ref[...] = (y * (gt / (1.0 + jnp.exp(-gt)))).astype(BF16)


def _gla_call(qg, kg, vg, gg, dec, gn512, bmask, hmask, bd512):
    C = GLA_CHUNK

    def blk(b, d, t):
        return (_gla_block(b, d, t), 0)

    def out_blk(b, d, t):
        return (_gla_block(b, 1, jnp.where(d == 0, 0, t)), 0)

    return pl.pallas_call(
        _gla_kernel,
        grid=(BATCH, 2, GLA_STEPS),
        in_specs=[
            pl.BlockSpec((C, GLA_QK_W), blk), pl.BlockSpec((C, GLA_QK_W), blk),
            pl.BlockSpec((C, GLA_V_W), blk),
            pl.BlockSpec((None, C, GLA_QK_W), lambda b, d, t: (d, _gla_block(b, d, t), 0)),
            pl.BlockSpec((C, GLA_V_W), out_blk),
            _full_spec((1, GLA_V_W), 3), _full_spec(bmask.shape, 3), _full_spec(hmask.shape, 3),
            _full_spec(bd512.shape, 3),
        ],
        out_specs=pl.BlockSpec((C, GLA_V_W), out_blk),
        out_shape=jax.ShapeDtypeStruct((NTOK, GLA_V_W), BF16),
        scratch_shapes=[pltpu.VMEM((GLA_QK_W, GLA_V_W), F32),
                        pltpu.VMEM((GLA_STEPS, C, GLA_V_W), F32)],
        compiler_params=_cparams("parallel", "arbitrary", "arbitrary"),
        name="gla_scan",
    )(qg, kg, vg, dec, gg, gn512, bmask, hmask, bd512)


def _mlp_kernel(ma_ref, mb_ref, wa_ref, wb_ref, x_ref, g1_ref, shift_ref, scale_ref, n2_ref,
                w1_ref, w3_ref, w2_ref, g2_ref, fg_ref, o_ref, *, final):
    hc = FFN_HIDDEN // MLP_HIDDEN_CHUNKS
    ox = _dot(ma_ref[...], wa_ref[...]) + _dot(mb_ref[...], wb_ref[...])
    x1 = x_ref[...] + g1_ref[...] * ox
    h = _norm_mod(x1, n2_ref, shift_ref, scale_ref)
    y = None
    for j in range(MLP_HIDDEN_CHUNKS):
        cols = slice(j * hc, (j + 1) * hc)
        a = _dot(h, w1_ref[:, cols])
        u = ((a / (1.0 + jnp.exp(-a))) * _dot(h, w3_ref[:, cols])).astype(BF16)
        part = _dot(u, w2_ref[cols, :])
        y = part if y is None else y + part
    x2 = x1 + g2_ref[...] * y
    if final:
        ms = jnp.mean(x2 * x2, axis=-1, keepdims=True)
        x2 = x2 * lax.rsqrt(ms + EPS) * fg_ref[...]
    o_ref[...] = x2


def _resident_spec(shape):
    zeros = (0,) * len(shape)
    return pl.BlockSpec(shape, lambda i: zeros, pipeline_mode=pl.Buffered(1))


def _mlp_call(mix_a, mix_b, col_b, w_a, w_b, xs, mod4, layer, norm_g, w1, w3, w2, final_g, n_tiles,
              final):
    half = D_MODEL // 2
    rows = n_tiles * ROW_TILE
    return pl.pallas_call(
        functools.partial(_mlp_kernel, final=final),
        grid=(n_tiles,),
        in_specs=[
            pl.BlockSpec((ROW_TILE, half), lambda i: (i, 0)),
            pl.BlockSpec((ROW_TILE, half), lambda i: (i, col_b)),
            _resident_spec((half, D_MODEL)), _resident_spec((half, D_MODEL)),
            _row_spec(D_MODEL),
            _mod_spec(layer, 2), _mod_spec(layer, 3), _mod_spec(layer, 4),
            _full_spec((1, D_MODEL)),
            _resident_spec(w1.shape), _resident_spec(w3.shape), _resident_spec(w2.shape),
            _mod_spec(layer, 5), _full_spec((1, D_MODEL)),
        ],
        out_specs=_row_spec(D_MODEL),
        out_shape=jax.ShapeDtypeStruct((rows, D_MODEL), F32),
        compiler_params=_cparams("parallel"),
        name="out_proj_ffn",
    )(mix_a, mix_b, w_a, w_b, xs, mod4, mod4, mod4, norm_g, w1, w3, w2, mod4, final_g)


def _slab_head_perm(n_heads):
    half = n_heads // 2
    cols = []
    for p in range(half):
        cols.append(np.arange(HEAD_DIM) + HEAD_DIM * p)
        cols.append(np.arange(HEAD_DIM) + HEAD_DIM * (half + p))
    return np.concatenate(cols)


def _rope_tables():
    pos = np.arange(SEQ)
    half = HEAD_DIM // 2
    inv = ROPE_THETA ** (-jnp.arange(0, half, 2, dtype=F32) / half)
    ang_r = (pos // GRID_W).astype(np.float32)[:, None] * inv[None]
    ang_c = (pos % GRID_W).astype(np.float32)[:, None] * inv[None]
    ang = jnp.concatenate([ang_r, ang_r, ang_c, ang_c] * 2, axis=-1)
    first = (np.arange(LANES) % 32) < 16
    cos, sin = jnp.cos(ang), jnp.sin(ang)
    sin_a = jnp.where(first[None], -sin, 0.0)
    sin_b = jnp.where(first[None], 0.0, sin)
    ident = jnp.ones((ROW_TILE, LANES), F32)
    zero = jnp.zeros((ROW_TILE, LANES), F32)
    return (jnp.concatenate([cos, ident]), jnp.concatenate([sin_a, zero]),
            jnp.concatenate([sin_b, zero]))


def _block_mean_matrix(width):
    idx = np.arange(width) // HEAD_DIM
    return jnp.asarray((idx[:, None] == idx[None, :]).astype(np.float32) / HEAD_DIM, BF16)


def kernel(x, c, ctx, c_ctx, ada_w, ada_b, norm1_g, norm2_g, ffn_w1, ffn_w3, ffn_w2,
           mixab_w_in, mixab_w_out, gla_gk_w, gla_gk_b, gla_norm_g, attn_qnorm_g, attn_knorm_g,
           win_w_in, win_w_out, win_sink, final_g):
    xs = jnp.concatenate([x.reshape(NX, D_MODEL), ctx.reshape(NC, D_MODEL)], axis=0)
    cvec = jnp.concatenate([c, c_ctx[None], jnp.zeros((MOD_ROWS - BATCH - 1, D_MODEL), F32)], axis=0)
    mod4 = _ada_call(cvec, ada_w, ada_b).reshape(DEPTH, MOD_ROWS, 1, 6 * D_MODEL)

    rope = _rope_tables()
    bd256 = _block_mean_matrix(2 * LANES)
    bd512 = _block_mean_matrix(GLA_V_W)
    head_of_k = np.arange(GLA_QK_W) // GLA_DK
    head_of_v = np.arange(GLA_V_W) // GLA_DV
    bmask = jnp.asarray((head_of_k[:, None] == head_of_v[None, :]).astype(np.float32))
    hmask = jnp.asarray((np.arange(GLA_HEADS)[:, None] == head_of_k[None, :]).astype(np.float32),
                        BF16).reshape(GLA_HEADS, 1, GLA_QK_W)
    perm8 = _slab_head_perm(B_HEADS)
    perm16 = _slab_head_perm(C_HEADS)
    final_row = final_g.reshape(1, D_MODEL)

    for l in range(DEPTH):
        with_ctx = l < DEPTH - 1
        i = l // 2
        n1 = norm1_g[l].reshape(1, D_MODEL)
        n2 = norm2_g[l].reshape(1, D_MODEL)
        if l % 2 == 0:
            w = mixab_w_in[i]
            w_lr = jnp.pad(w[:, 1536:1568], ((0, 0), (0, LANES - 2 * GLA_RANK)))
            w_main = jnp.concatenate([w[:, :1536], w[:, 1568:2080][:, perm8], w[:, 2080:], w_lr],
                                     axis=1).astype(BF16)
            w_gk = jnp.zeros((LANES, 2 * GLA_QK_W), F32)
            w_gk = w_gk.at[:GLA_RANK, :GLA_QK_W].set(gla_gk_w[i, 0])
            w_gk = w_gk.at[GLA_RANK:2 * GLA_RANK, GLA_QK_W:].set(gla_gk_w[i, 1]).astype(BF16)
            gk_bias = gla_gk_b[i].reshape(1, 2 * GLA_QK_W)
            qn = jnp.tile(attn_qnorm_g[i], 2).reshape(1, LANES)
            kn = jnp.tile(attn_knorm_g[i], 2).reshape(1, LANES)
            qg, kg, vg, gg, dec, bq, bk, bv = _even_in_call(
                xs, mod4, l, n1, w_main, w_gk, gk_bias, qn, kn, bd256, rope)
            gn512 = jnp.tile(gla_norm_g[i], GLA_HEADS).reshape(1, GLA_V_W)
            mix_a = _gla_call(qg, kg, vg, gg, dec, gn512, bmask, hmask, bd512)
            mix_b = _attn_call(bq, bk, bv, None, n_heads=B_HEADS, window=False, ctx_tiles=True)
            w_out = mixab_w_out[i]
            w_a = w_out[:GLA_V_W].astype(BF16)
            w_b = w_out[GLA_V_W:][perm8].astype(BF16)
            col_b = 0
        else:
            w = win_w_in[i]
            w_in = jnp.concatenate([w[:, :1024][:, perm16], w[:, 1024:]], axis=1).astype(BF16)
            q, k, v = _odd_in_call(xs, mod4, l, n1, w_in, rope)
            mix_a = _attn_call(q, k, v, win_sink[i], n_heads=C_HEADS, window=True,
                               ctx_tiles=with_ctx)
            mix_b = mix_a
            w_out = win_w_out[i][perm16].astype(BF16)
            w_a, w_b = w_out[:D_MODEL // 2], w_out[D_MODEL // 2:]
            col_b = 1
        n_tiles = N_ALL_TILES if with_ctx else N_X_TILES
        xs = _mlp_call(mix_a, mix_b, col_b, w_a, w_b, xs, mod4, l, n2, ffn_w1[l].astype(BF16),
                       ffn_w3[l].astype(BF16), ffn_w2[l].astype(BF16), final_row, n_tiles,
                       final=not with_ctx)
    return xs.reshape(BATCH, SEQ, D_MODEL)
```

```python
import functools

import numpy as np
import jax
import jax.numpy as jnp
from jax import lax
from jax.experimental import pallas as pl
from jax.experimental.pallas import tpu as pltpu

F32 = jnp.float32
BF16 = jnp.bfloat16

D_MODEL = 1024
BATCH = 8
SEQ = 2048
DEPTH = 4
GRID_W = 64
CTX_LEN = 256
HEAD_DIM = 64
ROPE_THETA = 10000.0
EPS = 1e-6
WINDOW = 128
GLA_HEADS = 8
GLA_DK = 32
GLA_DV = 64
GLA_RANK = 16
GLA_TAU = 16.0
GLA_QK_W = GLA_HEADS * GLA_DK
GLA_V_W = GLA_HEADS * GLA_DV
B_HEADS = 8
C_HEADS = 16
FFN_HIDDEN = 2816

NX = BATCH * SEQ
NC = BATCH * CTX_LEN
NTOK = NX + NC

LANES = 128
ROW_TILE = 512
IN_SUBTILES = 2
IN_SUB_ROWS = ROW_TILE // IN_SUBTILES
ATT_TQ = 256
GLA_CHUNK = 256
MLP_HIDDEN_CHUNKS = 11
VMEM_LIMIT = 52 * 1024 * 1024
NEG_BIG = -1e30
LOG2E = 1.4426950408889634
Q_SCALE = HEAD_DIM ** -0.5 * LOG2E

N_X_TILES = NX // ROW_TILE
N_ALL_TILES = NTOK // ROW_TILE
CTX_MOD_ROW = BATCH
MOD_ROWS = 16


def _cparams(*sem):
    return pltpu.CompilerParams(dimension_semantics=sem, vmem_limit_bytes=VMEM_LIMIT)


def _dot(a, b):
    return jnp.dot(a, b, preferred_element_type=F32)


def _dot_nt(a, b):
    return lax.dot_general(a, b, (((1,), (1,)), ((), ())), preferred_element_type=F32)


def _mod_row(i):
    return jnp.where(i < N_X_TILES, i // (SEQ // ROW_TILE), CTX_MOD_ROW)


def _mod_spec(layer, chunk):
    return pl.BlockSpec((None, None, 1, D_MODEL), lambda i: (layer, _mod_row(i), 0, chunk))


def _full_spec(shape):
    zeros = (0,) * len(shape)
    return pl.BlockSpec(shape, lambda i: zeros)


def _ada_kernel(s_ref, w_ref, b_ref, o_ref):
    s = s_ref[...]
    s = (s / (1.0 + jnp.exp(-s))).astype(BF16)
    o_ref[...] = _dot(s, w_ref[...].astype(BF16)) + b_ref[...]


def _ada_call(cvec, ada_w, ada_b):
    tn = 1536
    return pl.pallas_call(
        _ada_kernel,
        grid=(DEPTH, 6 * D_MODEL // tn),
        in_specs=[
            pl.BlockSpec((MOD_ROWS, D_MODEL), lambda l, j: (0, 0)),
            pl.BlockSpec((None, D_MODEL, tn), lambda l, j: (l, 0, j)),
            pl.BlockSpec((None, 1, tn), lambda l, j: (l, 0, j)),
        ],
        out_specs=pl.BlockSpec((None, MOD_ROWS, tn), lambda l, j: (l, 0, j)),
        out_shape=jax.ShapeDtypeStruct((DEPTH, MOD_ROWS, 6 * D_MODEL), F32),
        compiler_params=_cparams("parallel", "parallel"),
        name="ada_mod",
    )(cvec, ada_w, ada_b.reshape(DEPTH, 1, 6 * D_MODEL))


def _residual_rows(x_ref, xc_ref, rows):
    if xc_ref is None:
        return x_ref[rows, :]
    return jnp.where(pl.program_id(0) < N_X_TILES, x_ref[rows, :], xc_ref[rows, :])


def _kv_pair_layout(a, low):
    r = pltpu.roll(a, HEAD_DIM, 1)
    return jnp.where(low, a, r), jnp.where(low, r, a)


def _store_kv(k_ref, v_ref, rows, k, v):
    low = lax.broadcasted_iota(jnp.int32, k.shape, 1) < HEAD_DIM
    k0, k1 = _kv_pair_layout(k, low)
    v0, v1 = _kv_pair_layout(v, low)
    ones = jnp.ones(k.shape, BF16)
    k_ref[rows, 0:LANES] = k0.astype(BF16)
    k_ref[rows, LANES:2 * LANES] = k1.astype(BF16)
    v_ref[rows, 0:LANES] = v0.astype(BF16)
    v_ref[rows, LANES:2 * LANES] = ones
    v_ref[rows, 2 * LANES:3 * LANES] = v1.astype(BF16)
    v_ref[rows, 3 * LANES:4 * LANES] = ones


def _norm_mod(x, g_ref, shift_ref, scale_ref):
    ms = jnp.mean(x * x, axis=-1, keepdims=True)
    y = x * lax.rsqrt(ms + EPS) * g_ref[...]
    return (y * (1.0 + scale_ref[...]) + shift_ref[...]).astype(BF16)


def _rope(a, cos, sin_a, sin_b):
    return (a * cos + pltpu.roll(a, LANES - 16, 1) * sin_a + pltpu.roll(a, 16, 1) * sin_b)


def _even_in_kernel(*refs, split):
    x_ref, xc_ref = (refs[0], refs[1]) if split else (refs[0], None)
    (shift_ref, scale_ref, g_ref, w_ref, wg_ref, gb_ref, qn_ref, kn_ref, bd_ref, cos_ref, sa_ref,
     sb_ref, qg_ref, kg_ref, vg_ref, gg_ref, dec_ref, bq_ref, bk_ref, bv_ref) = refs[2 if split else 1:]
    for r in range(IN_SUBTILES):
        rows = pl.ds(r * IN_SUB_ROWS, IN_SUB_ROWS)
        h = _norm_mod(_residual_rows(x_ref, xc_ref, rows), g_ref, shift_ref, scale_ref)

        def proj(lo, hi):
            return _dot(h, w_ref[:, lo:hi])

        qg_ref[rows, :] = (proj(0, 256) * GLA_DK ** -0.5).astype(BF16)
        kg_ref[rows, :] = proj(256, 512).astype(BF16)
        vg_ref[rows, :] = proj(512, 1024).astype(BF16)
        gg_ref[rows, :] = proj(1024, 1536).astype(BF16)

        a = proj(1536, 2432)

        lr = a[:, 768:896].astype(BF16)
        z = _dot(lr, wg_ref[...]) + gb_ref[...]
        logsig = jnp.minimum(z, 0.0) - jnp.log(1.0 + jnp.exp(-jnp.abs(z)))
        dec = logsig * (1.0 / GLA_TAU)
        dec_ref[0, rows, :] = dec[:, :GLA_QK_W]
        dec_ref[1, rows, :] = dec[:, GLA_QK_W:]

        sq = (a[:, :640] * a[:, :640]).astype(BF16)
        bd = bd_ref[...]
        ms = [_dot(sq[:, 0:256], bd), _dot(sq[:, 256:512], bd),
              _dot(sq[:, 512:640], bd[:LANES, :LANES])]
        cos, sa, sb = cos_ref[rows, :], sa_ref[rows, :], sb_ref[rows, :]
        for p in range(5):
            g_row = qn_ref[...] if p < 4 else kn_ref[...]
            msp = ms[p // 2][:, LANES * (p % 2):LANES * (p % 2 + 1)]
            n = a[:, LANES * p:LANES * (p + 1)] * lax.rsqrt(msp + EPS) * g_row
            rp = _rope(n, cos, sa, sb)
            if p < 4:
                bq_ref[rows, LANES * p:LANES * (p + 1)] = (rp * Q_SCALE).astype(BF16)
            else:
                _store_kv(bk_ref, bv_ref, rows, rp, a[:, 640:768])


def _odd_in_kernel(x_ref, shift_ref, scale_ref, g_ref, w_ref, cos_ref, sa_ref, sb_ref,
                   q_ref, k_ref, v_ref):
    for r in range(IN_SUBTILES):
        rows = pl.ds(r * IN_SUB_ROWS, IN_SUB_ROWS)
        h = _norm_mod(x_ref[rows, :], g_ref, shift_ref, scale_ref)
        cos, sa, sb = cos_ref[rows, :], sa_ref[rows, :], sb_ref[rows, :]
        a = _dot(h, w_ref[...])
        for p in range(C_HEADS // 2):
            rp = _rope(a[:, LANES * p:LANES * (p + 1)], cos, sa, sb)
            q_ref[rows, LANES * p:LANES * (p + 1)] = (rp * Q_SCALE).astype(BF16)
        _store_kv(k_ref, v_ref, rows, _rope(a[:, 1024:1152], cos, sa, sb), a[:, 1152:1280])


def _rope_spec():
    per = SEQ // ROW_TILE
    return pl.BlockSpec((ROW_TILE, LANES), lambda i: (jnp.where(i < N_X_TILES, i % per, per), 0))


def _row_spec(width):
    return pl.BlockSpec((ROW_TILE, width), lambda i: (i, 0))


def _stream_specs(streams):
    if len(streams) == 1:
        return [_row_spec(D_MODEL)]
    return [pl.BlockSpec((ROW_TILE, D_MODEL), lambda i: (jnp.minimum(i, N_X_TILES - 1), 0)),
            pl.BlockSpec((ROW_TILE, D_MODEL), lambda i: (jnp.maximum(i - N_X_TILES, 0), 0))]


def _even_in_call(streams, mod4, layer, norm_g, w_main, w_gk, gk_bias, qn, kn, bd256, rope):
    outs = [(GLA_QK_W, BF16), (GLA_QK_W, BF16), (GLA_V_W, BF16), (GLA_V_W, BF16)]
    out_shape = [jax.ShapeDtypeStruct((NTOK, w), dt) for w, dt in outs]
    out_specs = [_row_spec(w) for w, _ in outs]
    out_shape.append(jax.ShapeDtypeStruct((2, NTOK, GLA_QK_W), F32))
    out_specs.append(pl.BlockSpec((2, ROW_TILE, GLA_QK_W), lambda i: (0, i, 0)))
    for w in (512, 2 * LANES, 4 * LANES):
        out_shape.append(jax.ShapeDtypeStruct((NTOK, w), BF16))
        out_specs.append(_row_spec(w))
    return pl.pallas_call(
        functools.partial(_even_in_kernel, split=len(streams) == 2),
        grid=(N_ALL_TILES,),
        in_specs=_stream_specs(streams) + [
            _mod_spec(layer, 0), _mod_spec(layer, 1),
            _full_spec((1, D_MODEL)), _full_spec(w_main.shape),
            _full_spec(w_gk.shape), _full_spec(gk_bias.shape),
            _full_spec((1, LANES)), _full_spec((1, LANES)), _full_spec((2 * LANES, 2 * LANES)),
            _rope_spec(), _rope_spec(), _rope_spec(),
        ],
        out_specs=out_specs,
        out_shape=out_shape,
        compiler_params=_cparams("parallel"),
        name="even_in_proj",
    )(*streams, mod4, mod4, norm_g, w_main, w_gk, gk_bias, qn, kn, bd256, *rope)


def _odd_in_call(xs, mod4, layer, norm_g, w_in, rope):
    widths = (C_HEADS * HEAD_DIM, 2 * LANES, 4 * LANES)
    return pl.pallas_call(
        _odd_in_kernel,
        grid=(N_ALL_TILES,),
        in_specs=[
            _row_spec(D_MODEL), _mod_spec(layer, 0), _mod_spec(layer, 1),
            _full_spec((1, D_MODEL)), _full_spec(w_in.shape),
            _rope_spec(), _rope_spec(), _rope_spec(),
        ],
        out_specs=[_row_spec(w) for w in widths],
        out_shape=[jax.ShapeDtypeStruct((NTOK, w), BF16) for w in widths],
        compiler_params=_cparams("parallel"),
        name="odd_in_proj",
    )(xs, mod4, mod4, norm_g, w_in, *rope)


def _attn_kernel(*refs, n_heads, window, has_sink, ctx_tiles):
    if has_sink:
        q_ref, kx_ref, vx_ref, kc_ref, vc_ref, sink_ref, o_ref = refs
    else:
        q_ref, kx_ref, vx_ref, kc_ref, vc_ref, o_ref = refs
        sink_ref = None
    tq = ATT_TQ
    t = pl.program_id(1)
    low = lax.broadcasted_iota(jnp.int32, (tq, LANES), 1) < HEAD_DIM
    zero = jnp.zeros((tq, LANES), BF16)

    def head(hd, keys, values, biases):
        p, half = hd // 2, hd % 2
        kv = hd // (n_heads // 2)
        q = q_ref[:, LANES * p:LANES * (p + 1)]
        qs = jnp.where(low, q, zero) if half == 0 else jnp.where(low, zero, q)
        ss = [_dot_nt(qs, k(kv)) if b is None else _dot_nt(qs, k(kv)) + b
              for k, b in zip(keys, biases)]
        m = functools.reduce(jnp.maximum, [jnp.max(x, axis=-1, keepdims=True) for x in ss])
        if has_sink:
            sk = sink_ref[hd] * LOG2E
            m = jnp.maximum(m, sk)
        acc = None
        for x, v in zip(ss, values):
            part = _dot(jnp.exp2(x - m).astype(BF16), v(kv))
            acc = part if acc is None else acc + part
        den = acc[:, LANES:]
        if has_sink:
            den = den + jnp.exp2(sk - m)
        lo = HEAD_DIM * half
        o_ref[:, HEAD_DIM * hd:HEAD_DIM * (hd + 1)] = (
            acc[:, lo:lo + HEAD_DIM] / den[:, lo:lo + HEAD_DIM]).astype(BF16)

    def kv_cols(ref, width, rows=slice(None)):
        return lambda kv: ref[rows, width * kv:width * (kv + 1)]

    def latent_tile():
        kc, vc = kv_cols(kc_ref, LANES), kv_cols(vc_ref, 2 * LANES)
        if window:
            band = tq + 2 * WINDOW
            start = pl.multiple_of(jnp.clip(t * tq - WINDOW, 0, SEQ - band), LANES)
            kb = kv_cols(kx_ref, LANES, pl.ds(start, band))
            vb = kv_cols(vx_ref, 2 * LANES, pl.ds(start, band))
            qpos = t * tq + lax.broadcasted_iota(jnp.int32, (tq, band), 0)
            kpos = start + lax.broadcasted_iota(jnp.int32, (tq, band), 1)
            bias = jnp.where(jnp.abs(qpos - kpos) <= WINDOW, 0.0, NEG_BIG).astype(F32)
            for hd in range(n_heads):
                head(hd, [kb, kc], [vb, vc], [bias, None])
        else:
            kx, vx = kv_cols(kx_ref, LANES), kv_cols(vx_ref, 2 * LANES)
            for hd in range(n_heads):
                head(hd, [kx, kc], [vx, vc], [None, None])

    def context_tile():
        kc, vc = kv_cols(kc_ref, LANES), kv_cols(vc_ref, 2 * LANES)
        for hd in range(n_heads):
            head(hd, [kc], [vc], [None])

    if ctx_tiles:
        pl.when(t < SEQ // tq)(latent_tile)
        pl.when(t >= SEQ // tq)(context_tile)
    else:
        latent_tile()


def _attn_call(q, k, v, sink, *, n_heads, window, ctx_tiles):
    tq = ATT_TQ
    n_xt = SEQ // tq
    n_ct = CTX_LEN // tq
    width = n_heads * HEAD_DIM
    has_sink = sink is not None

    def q_map(b, t):
        return (jnp.where(t < n_xt, b * n_xt + t, NX // tq + b * n_ct + (t - n_xt)), 0)

    in_specs = [
        pl.BlockSpec((tq, width), q_map),
        pl.BlockSpec((SEQ, 2 * LANES), lambda b, t: (b, 0)),
        pl.BlockSpec((SEQ, 4 * LANES), lambda b, t: (b, 0)),
        pl.BlockSpec((CTX_LEN, 2 * LANES), lambda b, t: (NX // CTX_LEN + b, 0)),
        pl.BlockSpec((CTX_LEN, 4 * LANES), lambda b, t: (NX // CTX_LEN + b, 0)),
    ]
    args = [q, k, v, k, v]
    if has_sink:
        in_specs.append(pl.BlockSpec(memory_space=pltpu.SMEM))
        args.append(sink)
    rows = NTOK if ctx_tiles else NX
    return pl.pallas_call(
        functools.partial(_attn_kernel, n_heads=n_heads, window=window, has_sink=has_sink,
                          ctx_tiles=ctx_tiles),
        grid=(BATCH, n_xt + (n_ct if ctx_tiles else 0)),
        in_specs=in_specs,
        out_specs=pl.BlockSpec((tq, width), q_map),
        out_shape=jax.ShapeDtypeStruct((rows, width), BF16),
        compiler_params=_cparams("parallel", "arbitrary"),
        name="win_attn" if window else "dense_attn",
    )(*args)


GLA_N_CTX = CTX_LEN // GLA_CHUNK
GLA_N_X = SEQ // GLA_CHUNK
GLA_STEPS = GLA_N_CTX + GLA_N_X
GLA_SAFE_LOG_DECAY = 60.0


def _gla_pos_block(b, pos):
    return jnp.where(pos < GLA_N_CTX, NX // GLA_CHUNK + b * GLA_N_CTX + pos,
                     b * GLA_N_X + pos - GLA_N_CTX)


def _gla_pos(t, fwd):
    if fwd:
        return t
    return jnp.where(t < GLA_N_CTX, GLA_N_CTX - 1 - t, GLA_STEPS + GLA_N_CTX - 1 - t)


def _bf16_pieces(x, n):
    pieces = []
    for _ in range(n):
        p = x.astype(BF16)
        pieces.append(p)
        x = x - p.astype(F32)
    return pieces


def _gla_intra_by_lag(q_ref, k_ref, v_ref, g_ref, tri, bmask_bf, oi_scr, fwd):
    C = GLA_CHUNK
    bc = sum(_dot(tri, piece) for piece in _bf16_pieces(g_ref[...], 3))
    b_pieces = _bf16_pieces(bc, 3)
    qf = q_ref[...].astype(F32)
    k, v = k_ref[...], v_ref[...]
    diff = (lax.broadcasted_iota(jnp.int32, (C, C), 0) - lax.broadcasted_iota(jnp.int32, (C, C), 1))
    if not fwd:
        diff = -diff

    def body(lag, acc):
        shift = (diff == lag).astype(F32).astype(BF16)
        bs = sum(_dot(shift, piece) for piece in b_pieces)
        w = qf * _dot(shift, k) * jnp.exp(jnp.minimum(bc - bs, 0.0))
        return acc + _dot(w.astype(BF16), bmask_bf) * _dot(shift, v)

    oi_scr[...] = lax.fori_loop(0, C, body, jnp.zeros((C, GLA_V_W), F32))


def _gla_chunk(q_ref, k_ref, v_ref, g_ref, tri, bmask_ref, hmask_ref, s_scr, oi_scr, slow, fwd):
    C = GLA_CHUNK
    bc = sum(_dot(tri, piece) for piece in _bf16_pieces(g_ref[...], 2))
    tot = bc[C - 1:C, :] if fwd else bc[0:1, :]
    qf = q_ref[...].astype(F32)
    kf = k_ref[...].astype(F32)
    v = v_ref[...]
    qb = (qf * jnp.exp(bc)).astype(BF16)
    s_old = s_scr[...]
    o = _dot(qb, s_old.astype(BF16))

    kp = (kf * jnp.exp(-bc)).astype(BF16)
    qst = jnp.concatenate([qb * hmask_ref[h] for h in range(GLA_HEADS)], axis=0)
    s = _dot_nt(qst, kp).astype(BF16)
    low = lax.broadcasted_iota(jnp.int32, (C, LANES), 1) < GLA_DV
    parts = []
    for p in range(GLA_HEADS // 2):
        a = jnp.concatenate([s[2 * p * C:(2 * p + 1) * C] * tri, s[(2 * p + 1) * C:(2 * p + 2) * C] * tri],
                            axis=0)
        r = _dot(a, v[:, LANES * p:LANES * (p + 1)])
        parts.append(jnp.where(low, r[:C], r[C:]))
    o_intra = jnp.concatenate(parts, axis=1)
    o = o + jnp.where(slow, oi_scr[...], o_intra)

    kt = (kf * jnp.exp(tot - bc)).T.astype(BF16)
    kv = _dot(kt, v)
    dcol = jnp.exp(jnp.broadcast_to(tot, (LANES, GLA_QK_W))).T
    dmat = jnp.concatenate([dcol] * (GLA_V_W // LANES), axis=1)
    s_scr[...] = dmat * s_old + bmask_ref[...] * kv
    return o


def _gla_kernel(qf_ref, kf_ref, vf_ref, gf_ref, qb_ref, kb_ref, vb_ref, gb_ref, gate_ref, gn_ref,
                bmask_ref, hmask_ref, bd_ref, trif_ref, trib_ref,
                o_ref, sf_scr, sb_scr, of_scr, ob_scr, oif_scr, oib_scr):
    t = pl.program_id(1)

    @pl.when(t == 0)
    def _():
        sf_scr[...] = jnp.zeros_like(sf_scr)
        sb_scr[...] = jnp.zeros_like(sb_scr)
        oif_scr[...] = jnp.zeros_like(oif_scr)
        oib_scr[...] = jnp.zeros_like(oib_scr)

    @pl.when(t < GLA_STEPS)
    def _():
        slow_f = jnp.min(jnp.sum(gf_ref[...], axis=0, keepdims=True)) < -GLA_SAFE_LOG_DECAY
        slow_b = jnp.min(jnp.sum(gb_ref[...], axis=0, keepdims=True)) < -GLA_SAFE_LOG_DECAY
        bmask_bf = bmask_ref[...].astype(BF16)

        @pl.when(slow_f)
        def _():
            _gla_intra_by_lag(qf_ref, kf_ref, vf_ref, gf_ref, trif_ref[...], bmask_bf, oif_scr, True)

        @pl.when(slow_b)
        def _():
            _gla_intra_by_lag(qb_ref, kb_ref, vb_ref, gb_ref, trib_ref[...], bmask_bf, oib_scr, False)

        of_scr[_gla_pos(t, True)] = _gla_chunk(
            qf_ref, kf_ref, vf_ref, gf_ref, trif_ref[...], bmask_ref, hmask_ref, sf_scr, oif_scr,
            slow_f, True)
        ob_scr[_gla_pos(t, False)] = _gla_chunk(
            qb_ref, kb_ref, vb_ref, gb_ref, trib_ref[...], bmask_ref, hmask_ref, sb_scr, oib_scr,
            slow_b, False)

    @pl.when(t >= GLA_STEPS)
    def _():
        pos = t - GLA_STEPS
        ot = of_scr[pos] + ob_scr[pos]
        ms = _dot((ot * ot).astype(BF16), bd_ref[...])
        y = ot * lax.rsqrt(ms + EPS) * gn_ref[...]
        gt = gate_ref[...].astype(F32)
        o_ref[...] = (y * (gt / (1.0 + jnp.exp(-gt)))).astype(BF16)


def _gla_call(qg, kg, vg, gg, dec, gn512, bmask, hmask, bd512, trif, trib):
    C = GLA_CHUNK
    last = GLA_STEPS - 1

    def scan_blk(fwd):
        return lambda b, t: (_gla_pos_block(b, _gla_pos(jnp.minimum(t, last), fwd)), 0)

    def dec_blk(fwd):
        return lambda b, t: (0 if fwd else 1,
                             _gla_pos_block(b, _gla_pos(jnp.minimum(t, last), fwd)), 0)

    def out_blk(b, t):
        return (_gla_pos_block(b, jnp.maximum(t - GLA_STEPS, 0)), 0)

    def chain_specs(fwd):
        return [pl.BlockSpec((C, GLA_QK_W), scan_blk(fwd)), pl.BlockSpec((C, GLA_QK_W), scan_blk(fwd)),
                pl.BlockSpec((C, GLA_V_W), scan_blk(fwd)),
                pl.BlockSpec((None, C, GLA_QK_W), dec_blk(fwd))]

    def const(shape):
        zeros = (0,) * len(shape)
        return pl.BlockSpec(shape, lambda b, t: zeros)

    return pl.pallas_call(
        _gla_kernel,
        grid=(BATCH, 2 * GLA_STEPS),
        in_specs=chain_specs(True) + chain_specs(False) + [
            pl.BlockSpec((C, GLA_V_W), out_blk),
            const((1, GLA_V_W)), const(bmask.shape), const(hmask.shape), const(bd512.shape),
            const((C, C)), const((C, C)),
        ],
        out_specs=pl.BlockSpec((C, GLA_V_W), out_blk),
        out_shape=jax.ShapeDtypeStruct((NTOK, GLA_V_W), BF16),
        scratch_shapes=[pltpu.VMEM((GLA_QK_W, GLA_V_W), F32), pltpu.VMEM((GLA_QK_W, GLA_V_W), F32),
                        pltpu.VMEM((GLA_STEPS, C, GLA_V_W), F32),
                        pltpu.VMEM((GLA_STEPS, C, GLA_V_W), F32),
                        pltpu.VMEM((C, GLA_V_W), F32), pltpu.VMEM((C, GLA_V_W), F32)],
        compiler_params=_cparams("parallel", "arbitrary"),
        name="gla_scan",
    )(qg, kg, vg, dec, qg, kg, vg, dec, gg, gn512, bmask, hmask, bd512, trif, trib)


def _mlp_kernel(*refs, final, split):
    x_ref, xc_ref = (refs[0], refs[1]) if split else (refs[0], None)
    (ma_ref, mb_ref, wa_ref, wb_ref, g1_ref, shift_ref, scale_ref, n2_ref,
     w1_ref, w3_ref, w2_ref, g2_ref, fg_ref, o_ref) = refs[2 if split else 1:]
    hc = FFN_HIDDEN // MLP_HIDDEN_CHUNKS
    ox = _dot(ma_ref[...], wa_ref[...]) + _dot(mb_ref[...], wb_ref[...])
    x1 = _residual_rows(x_ref, xc_ref, slice(None)) + g1_ref[...] * ox
    h = _norm_mod(x1, n2_ref, shift_ref, scale_ref)
    y = None
    for j in range(MLP_HIDDEN_CHUNKS):
        cols = slice(j * hc, (j + 1) * hc)
        a = _dot(h, w1_ref[:, cols])
        u = ((a / (1.0 + jnp.exp(-a))) * _dot(h, w3_ref[:, cols])).astype(BF16)
        part = _dot(u, w2_ref[cols, :])
        y = part if y is None else y + part
    x2 = x1 + g2_ref[...] * y
    if final:
        ms = jnp.mean(x2 * x2, axis=-1, keepdims=True)
        x2 = x2 * lax.rsqrt(ms + EPS) * fg_ref[...]
    o_ref[...] = x2


def _resident_spec(shape):
    zeros = (0,) * len(shape)
    return pl.BlockSpec(shape, lambda i: zeros, pipeline_mode=pl.Buffered(1))


def _mlp_call(mix_a, mix_b, col_b, w_a, w_b, streams, mod4, layer, norm_g, w1, w3, w2, final_g,
              n_tiles, final):
    half = D_MODEL // 2
    rows = n_tiles * ROW_TILE
    return pl.pallas_call(
        functools.partial(_mlp_kernel, final=final, split=len(streams) == 2),
        grid=(n_tiles,),
        in_specs=_stream_specs(streams) + [
            pl.BlockSpec((ROW_TILE, half), lambda i: (i, 0)),
            pl.BlockSpec((ROW_TILE, half), lambda i: (i, col_b)),
            _resident_spec((half, D_MODEL)), _resident_spec((half, D_MODEL)),
            _mod_spec(layer, 2), _mod_spec(layer, 3), _mod_spec(layer, 4),
            _full_spec((1, D_MODEL)),
            _resident_spec(w1.shape), _resident_spec(w3.shape), _resident_spec(w2.shape),
            _mod_spec(layer, 5), _full_spec((1, D_MODEL)),
        ],
        out_specs=_row_spec(D_MODEL),
        out_shape=jax.ShapeDtypeStruct((rows, D_MODEL), F32),
        compiler_params=_cparams("parallel"),
        name="out_proj_ffn",
    )(*streams, mix_a, mix_b, w_a, w_b, mod4, mod4, mod4, norm_g, w1, w3, w2, mod4, final_g)


def _rope_tables():
    pos = np.arange(SEQ)
    half = HEAD_DIM // 2
    inv = ROPE_THETA ** (-jnp.arange(0, half, 2, dtype=F32) / half)
    ang_r = (pos // GRID_W).astype(np.float32)[:, None] * inv[None]
    ang_c = (pos % GRID_W).astype(np.float32)[:, None] * inv[None]
    ang = jnp.concatenate([ang_r, ang_r, ang_c, ang_c] * 2, axis=-1)
    first = (np.arange(LANES) % 32) < 16
    cos, sin = jnp.cos(ang), jnp.sin(ang)
    sin_a = jnp.where(first[None], -sin, 0.0)
    sin_b = jnp.where(first[None], 0.0, sin)
    ident = jnp.ones((ROW_TILE, LANES), F32)
    zero = jnp.zeros((ROW_TILE, LANES), F32)
    return (jnp.concatenate([cos, ident]), jnp.concatenate([sin_a, zero]),
            jnp.concatenate([sin_b, zero]))


def _block_mean_matrix(width):
    idx = np.arange(width) // HEAD_DIM
    return jnp.asarray((idx[:, None] == idx[None, :]).astype(np.float32) / HEAD_DIM, BF16)


def kernel(x, c, ctx, c_ctx, ada_w, ada_b, norm1_g, norm2_g, ffn_w1, ffn_w3, ffn_w2,
           mixab_w_in, mixab_w_out, gla_gk_w, gla_gk_b, gla_norm_g, attn_qnorm_g, attn_knorm_g,
           win_w_in, win_w_out, win_sink, final_g):
    streams = (x.reshape(NX, D_MODEL), ctx.reshape(NC, D_MODEL))
    cvec = jnp.concatenate([c, c_ctx[None], jnp.zeros((MOD_ROWS - BATCH - 1, D_MODEL), F32)], axis=0)
    mod4 = _ada_call(cvec, ada_w, ada_b).reshape(DEPTH, MOD_ROWS, 1, 6 * D_MODEL)

    rope = _rope_tables()
    bd256 = _block_mean_matrix(2 * LANES)
    bd512 = _block_mean_matrix(GLA_V_W)
    head_of_k = np.arange(GLA_QK_W) // GLA_DK
    head_of_v = np.arange(GLA_V_W) // GLA_DV
    bmask = jnp.asarray((head_of_k[:, None] == head_of_v[None, :]).astype(np.float32))
    hmask = jnp.asarray((np.arange(GLA_HEADS)[:, None] == head_of_k[None, :]).astype(np.float32),
                        BF16).reshape(GLA_HEADS, 1, GLA_QK_W)
    tri_np = np.tril(np.ones((GLA_CHUNK, GLA_CHUNK), np.float32))
    trif, trib = jnp.asarray(tri_np, BF16), jnp.asarray(tri_np.T, BF16)
    final_row = final_g.reshape(1, D_MODEL)

    for l in range(DEPTH):
        with_ctx = l < DEPTH - 1
        i = l // 2
        n1 = norm1_g[l].reshape(1, D_MODEL)
        n2 = norm2_g[l].reshape(1, D_MODEL)
        if l % 2 == 0:
            w = mixab_w_in[i]
            w_lr = jnp.pad(w[:, 1536:1568], ((0, 0), (0, LANES - 2 * GLA_RANK)))
            w_main = jnp.concatenate([w[:, :1536], w[:, 1568:], w_lr], axis=1).astype(BF16)
            w_gk = jnp.zeros((LANES, 2 * GLA_QK_W), F32)
            w_gk = w_gk.at[:GLA_RANK, :GLA_QK_W].set(gla_gk_w[i, 0])
            w_gk = w_gk.at[GLA_RANK:2 * GLA_RANK, GLA_QK_W:].set(gla_gk_w[i, 1]).astype(BF16)
            gk_bias = gla_gk_b[i].reshape(1, 2 * GLA_QK_W)
            qn = jnp.tile(attn_qnorm_g[i], 2).reshape(1, LANES)
            kn = jnp.tile(attn_knorm_g[i], 2).reshape(1, LANES)
            qg, kg, vg, gg, dec, bq, bk, bv = _even_in_call(
                streams, mod4, l, n1, w_main, w_gk, gk_bias, qn, kn, bd256, rope)
            gn512 = jnp.tile(gla_norm_g[i], GLA_HEADS).reshape(1, GLA_V_W)
            mix_a = _gla_call(qg, kg, vg, gg, dec, gn512, bmask, hmask, bd512, trif, trib)
            mix_b = _attn_call(bq, bk, bv, None, n_heads=B_HEADS, window=False, ctx_tiles=True)
            w_out = mixab_w_out[i]
            w_a = w_out[:GLA_V_W].astype(BF16)
            w_b = w_out[GLA_V_W:].astype(BF16)
            col_b = 0
        else:
            q, k, v = _odd_in_call(streams[0], mod4, l, n1, win_w_in[i].astype(BF16), rope)
            mix_a = _attn_call(q, k, v, win_sink[i], n_heads=C_HEADS, window=True,
                               ctx_tiles=with_ctx)
            mix_b = mix_a
            w_out = win_w_out[i].astype(BF16)
            w_a, w_b = w_out[:D_MODEL // 2], w_out[D_MODEL // 2:]
            col_b = 1
        n_tiles = N_ALL_TILES if with_ctx else N_X_TILES
        streams = (_mlp_call(mix_a, mix_b, col_b, w_a, w_b, streams, mod4, l, n2,
                             ffn_w1[l].astype(BF16), ffn_w3[l].astype(BF16), ffn_w2[l].astype(BF16),
                             final_row, n_tiles, final=not with_ctx),)
    return streams[0].reshape(BATCH, SEQ, D_MODEL)
```

```python
import functools

import numpy as np
import jax
import jax.numpy as jnp
from jax import lax
from jax.experimental import pallas as pl
from jax.experimental.pallas import tpu as pltpu

F32 = jnp.float32
BF16 = jnp.bfloat16

D_MODEL = 1024
BATCH = 8
SEQ = 2048
DEPTH = 4
GRID_W = 64
CTX_LEN = 256
HEAD_DIM = 64
ROPE_THETA = 10000.0
EPS = 1e-6
WINDOW = 128
GLA_HEADS = 8
GLA_DK = 32
GLA_DV = 64
GLA_RANK = 16
GLA_TAU = 16.0
GLA_QK_W = GLA_HEADS * GLA_DK
GLA_V_W = GLA_HEADS * GLA_DV
B_HEADS = 8
C_HEADS = 16
FFN_HIDDEN = 2816

NX = BATCH * SEQ
NC = BATCH * CTX_LEN
NTOK = NX + NC

LANES = 128
ROW_TILE = 512
IN_ROW_TILE = 1024
IN_SUBTILES = 4
IN_SUB_ROWS = IN_ROW_TILE // IN_SUBTILES
ATT_SCORES_AHEAD = {False: 3, True: 4}
ATT_TQ = 256
GLA_CHUNK = 256
MLP_HIDDEN_CHUNKS = 11
VMEM_LIMIT = 52 * 1024 * 1024
NEG_BIG = -1e30
LOG2E = 1.4426950408889634
Q_SCALE = HEAD_DIM ** -0.5 * LOG2E

N_X_TILES = NX // ROW_TILE
N_ALL_TILES = NTOK // ROW_TILE
CTX_MOD_ROW = BATCH
MOD_ROWS = 16


def _cparams(*sem):
    return pltpu.CompilerParams(dimension_semantics=sem, vmem_limit_bytes=VMEM_LIMIT)


def _dot(a, b):
    return jnp.dot(a, b, preferred_element_type=F32)


def _dot_nt(a, b):
    return lax.dot_general(a, b, (((1,), (1,)), ((), ())), preferred_element_type=F32)


def _mod_spec(layer, chunk, tile=ROW_TILE):
    def row(i):
        return jnp.where(i < NX // tile, i // (SEQ // tile), CTX_MOD_ROW)
    return pl.BlockSpec((None, None, 1, D_MODEL), lambda i: (layer, row(i), 0, chunk))


def _full_spec(shape):
    zeros = (0,) * len(shape)
    return pl.BlockSpec(shape, lambda i: zeros)


def _ada_kernel(s_ref, w_ref, b_ref, o_ref):
    s = s_ref[...]
    s = (s / (1.0 + jnp.exp(-s))).astype(BF16)
    o_ref[...] = _dot(s, w_ref[...].astype(BF16)) + b_ref[...]


def _ada_call(cvec, ada_w, ada_b):
    tn = 1536
    return pl.pallas_call(
        _ada_kernel,
        grid=(DEPTH, 6 * D_MODEL // tn),
        in_specs=[
            pl.BlockSpec((MOD_ROWS, D_MODEL), lambda l, j: (0, 0)),
            pl.BlockSpec((None, D_MODEL, tn), lambda l, j: (l, 0, j)),
            pl.BlockSpec((None, 1, tn), lambda l, j: (l, 0, j)),
        ],
        out_specs=pl.BlockSpec((None, MOD_ROWS, tn), lambda l, j: (l, 0, j)),
        out_shape=jax.ShapeDtypeStruct((DEPTH, MOD_ROWS, 6 * D_MODEL), F32),
        compiler_params=_cparams("parallel", "parallel"),
        name="ada_mod",
    )(cvec, ada_w, ada_b.reshape(DEPTH, 1, 6 * D_MODEL))


def _residual_rows(x_ref, xc_ref, rows, tile=ROW_TILE):
    if xc_ref is None:
        return x_ref[rows, :]
    return jnp.where(pl.program_id(0) < NX // tile, x_ref[rows, :], xc_ref[rows, :])


def _kv_pair_layout(a, low):
    r = pltpu.roll(a, HEAD_DIM, 1)
    return jnp.where(low, a, r), jnp.where(low, r, a)


def _store_kv(k_ref, vt_ref, rows, k, v):
    low = lax.broadcasted_iota(jnp.int32, k.shape, 1) < HEAD_DIM
    k0, k1 = _kv_pair_layout(k, low)
    k_ref[rows, 0:LANES] = k0.astype(BF16)
    k_ref[rows, LANES:2 * LANES] = k1.astype(BF16)
    vt_ref[:, rows] = v.T.astype(BF16)


def _norm_mod(x, g_ref, shift_ref, scale_ref):
    ms = jnp.mean(x * x, axis=-1, keepdims=True)
    y = x * lax.rsqrt(ms + EPS) * g_ref[...]
    return (y * (1.0 + scale_ref[...]) + shift_ref[...]).astype(BF16)


def _rope(a, cos, sin_a, sin_b):
    return (a * cos + pltpu.roll(a, LANES - 16, 1) * sin_a + pltpu.roll(a, 16, 1) * sin_b)


def _even_in_kernel(*refs, split):
    x_ref, xc_ref = (refs[0], refs[1]) if split else (refs[0], None)
    (shift_ref, scale_ref, g_ref, w_ref, wg_ref, gb_ref, qn_ref, kn_ref, bd_ref, cos_ref, sa_ref,
     sb_ref, qg_ref, kg_ref, vg_ref, gg_ref, dec_ref, bq_ref, bk_ref, bv_ref) = refs[2 if split else 1:]
    for r in range(IN_SUBTILES):
        rows = pl.ds(r * IN_SUB_ROWS, IN_SUB_ROWS)
        h = _norm_mod(_residual_rows(x_ref, xc_ref, rows, IN_ROW_TILE), g_ref, shift_ref, scale_ref)

        def proj(lo, hi):
            return _dot(h, w_ref[:, lo:hi])

        qg_ref[rows, :] = (proj(0, 256) * GLA_DK ** -0.5).astype(BF16)
        kg_ref[rows, :] = proj(256, 512).astype(BF16)
        vg_ref[rows, :] = proj(512, 1024).astype(BF16)
        gg_ref[rows, :] = proj(1024, 1536).astype(BF16)

        a = proj(1536, 2432)

        lr = a[:, 768:896].astype(BF16)
        z = _dot(lr, wg_ref[...]) + gb_ref[...]
        logsig = jnp.minimum(z, 0.0) - jnp.log(1.0 + jnp.exp(-jnp.abs(z)))
        dec = logsig * (1.0 / GLA_TAU)
        dec_ref[0, rows, :] = dec[:, :GLA_QK_W]
        dec_ref[1, rows, :] = dec[:, GLA_QK_W:]

        sq = (a[:, :640] * a[:, :640]).astype(BF16)
        bd = bd_ref[...]
        ms = [_dot(sq[:, 0:256], bd), _dot(sq[:, 256:512], bd),
              _dot(sq[:, 512:640], bd[:LANES, :LANES])]
        cos, sa, sb = cos_ref[rows, :], sa_ref[rows, :], sb_ref[rows, :]
        for p in range(5):
            g_row = qn_ref[...] if p < 4 else kn_ref[...]
            msp = ms[p // 2][:, LANES * (p % 2):LANES * (p % 2 + 1)]
            n = a[:, LANES * p:LANES * (p + 1)] * lax.rsqrt(msp + EPS) * g_row
            rp = _rope(n, cos, sa, sb)
            if p < 4:
                bq_ref[rows, LANES * p:LANES * (p + 1)] = (rp * Q_SCALE).astype(BF16)
            else:
                _store_kv(bk_ref, bv_ref, rows, rp, a[:, 640:768])


def _odd_in_kernel(x_ref, shift_ref, scale_ref, g_ref, w_ref, cos_ref, sa_ref, sb_ref,
                   q_ref, k_ref, v_ref):
    for r in range(IN_SUBTILES):
        rows = pl.ds(r * IN_SUB_ROWS, IN_SUB_ROWS)
        h = _norm_mod(x_ref[rows, :], g_ref, shift_ref, scale_ref)
        cos, sa, sb = cos_ref[rows, :], sa_ref[rows, :], sb_ref[rows, :]
        a = _dot(h, w_ref[...])
        for p in range(C_HEADS // 2):
            rp = _rope(a[:, LANES * p:LANES * (p + 1)], cos, sa, sb)
            q_ref[rows, LANES * p:LANES * (p + 1)] = (rp * Q_SCALE).astype(BF16)
        _store_kv(k_ref, v_ref, rows, _rope(a[:, 1024:1152], cos, sa, sb), a[:, 1152:1280])


def _rope_spec():
    per = SEQ // IN_ROW_TILE
    return pl.BlockSpec((IN_ROW_TILE, LANES),
                        lambda i: (jnp.where(i < NX // IN_ROW_TILE, i % per, per), 0))


def _vt_spec():
    return pl.BlockSpec((LANES, IN_ROW_TILE), lambda i: (0, i))


def _row_spec(width, tile=ROW_TILE):
    return pl.BlockSpec((tile, width), lambda i: (i, 0))


def _stream_specs(streams, tile=ROW_TILE):
    if len(streams) == 1:
        return [_row_spec(D_MODEL, tile)]
    n_x = NX // tile
    return [pl.BlockSpec((tile, D_MODEL), lambda i: (jnp.minimum(i, n_x - 1), 0)),
            pl.BlockSpec((tile, D_MODEL), lambda i: (jnp.maximum(i - n_x, 0), 0))]


def _even_in_call(streams, mod4, layer, norm_g, w_main, w_gk, gk_bias, qn, kn, bd256, rope):
    outs = [(GLA_QK_W, BF16), (GLA_QK_W, BF16), (GLA_V_W, BF16), (GLA_V_W, BF16)]
    out_shape = [jax.ShapeDtypeStruct((NTOK, w), dt) for w, dt in outs]
    out_specs = [_row_spec(w, IN_ROW_TILE) for w, _ in outs]
    out_shape.append(jax.ShapeDtypeStruct((2, NTOK, GLA_QK_W), F32))
    out_specs.append(pl.BlockSpec((2, IN_ROW_TILE, GLA_QK_W), lambda i: (0, i, 0)))
    for w in (512, 2 * LANES):
        out_shape.append(jax.ShapeDtypeStruct((NTOK, w), BF16))
        out_specs.append(_row_spec(w, IN_ROW_TILE))
    out_shape.append(jax.ShapeDtypeStruct((LANES, NTOK), BF16))
    out_specs.append(_vt_spec())
    return pl.pallas_call(
        functools.partial(_even_in_kernel, split=len(streams) == 2),
        grid=(NTOK // IN_ROW_TILE,),
        in_specs=_stream_specs(streams, IN_ROW_TILE) + [
            _mod_spec(layer, 0, IN_ROW_TILE), _mod_spec(layer, 1, IN_ROW_TILE),
            _full_spec((1, D_MODEL)), _full_spec(w_main.shape),
            _full_spec(w_gk.shape), _full_spec(gk_bias.shape),
            _full_spec((1, LANES)), _full_spec((1, LANES)), _full_spec((2 * LANES, 2 * LANES)),
            _rope_spec(), _rope_spec(), _rope_spec(),
        ],
        out_specs=out_specs,
        out_shape=out_shape,
        compiler_params=_cparams("parallel"),
        name="even_in_proj",
    )(*streams, mod4, mod4, norm_g, w_main, w_gk, gk_bias, qn, kn, bd256, *rope)


def _odd_in_call(xs, mod4, layer, norm_g, w_in, rope):
    widths = (C_HEADS * HEAD_DIM, 2 * LANES)
    return pl.pallas_call(
        _odd_in_kernel,
        grid=(NTOK // IN_ROW_TILE,),
        in_specs=[
            _row_spec(D_MODEL, IN_ROW_TILE), _mod_spec(layer, 0, IN_ROW_TILE),
            _mod_spec(layer, 1, IN_ROW_TILE),
            _full_spec((1, D_MODEL)), _full_spec(w_in.shape),
            _rope_spec(), _rope_spec(), _rope_spec(),
        ],
        out_specs=[_row_spec(w, IN_ROW_TILE) for w in widths] + [_vt_spec()],
        out_shape=[jax.ShapeDtypeStruct((NTOK, w), BF16) for w in widths]
        + [jax.ShapeDtypeStruct((LANES, NTOK), BF16)],
        compiler_params=_cparams("parallel"),
        name="odd_in_proj",
    )(xs, mod4, mod4, norm_g, w_in, *rope)


def _attn_kernel(*refs, n_heads, window, has_sink, ctx_tiles):
    if has_sink:
        q_ref, kx_ref, vx_ref, kc_ref, vc_ref, sink_ref, o_ref = refs
    else:
        q_ref, kx_ref, vx_ref, kc_ref, vc_ref, o_ref = refs
        sink_ref = None
    tq = ATT_TQ
    t = pl.program_id(1)
    low = lax.broadcasted_iota(jnp.int32, (tq, LANES), 1) < HEAD_DIM
    zero = jnp.zeros((tq, LANES), BF16)

    def scores(hd, keys, biases):
        p, half = hd // 2, hd % 2
        kv = hd // (n_heads // 2)
        q = q_ref[:, LANES * p:LANES * (p + 1)]
        qs = jnp.where(low, q, zero) if half == 0 else jnp.where(low, zero, q)
        return [_dot_nt(k(kv), qs).astype(BF16) if b is None else _dot_nt(k(kv), qs).astype(BF16) + b
                for k, b in zip(keys, biases)]

    def finish(hd, ss, values):
        kv = hd // (n_heads // 2)
        m = functools.reduce(jnp.maximum, [jnp.max(x, axis=0, keepdims=True) for x in ss])
        if has_sink:
            sk = (sink_ref[hd] * LOG2E).astype(BF16)
            m = jnp.maximum(m, sk)
        acc = None
        for x, v in zip(ss, values):
            part = _dot(v(kv), jnp.exp2(x - m))
            acc = part if acc is None else acc + part
        den = acc[HEAD_DIM:HEAD_DIM + 1, :]
        if has_sink:
            den = den + jnp.exp2(sk - m).astype(F32)
        return acc[:HEAD_DIM, :] / den

    def all_heads(keys, values, biases):
        ahead = ATT_SCORES_AHEAD[window]
        pending = [scores(hd, keys, biases) for hd in range(ahead)]
        outs = []
        for hd in range(n_heads):
            if hd + ahead < n_heads:
                pending.append(scores(hd + ahead, keys, biases))
            outs.append(finish(hd, pending.pop(0), values))
            if hd % 2 == 1:
                ot = jnp.concatenate(outs, axis=0)
                o_ref[:, LANES * (hd // 2):LANES * (hd // 2 + 1)] = ot.T.astype(BF16)
                outs = []

    def k_cols(ref, rows=slice(None)):
        return lambda kv: ref[rows, LANES * kv:LANES * (kv + 1)]

    def v_rows(ref, cols=slice(None)):
        def get(kv):
            vt = ref[HEAD_DIM * kv:HEAD_DIM * (kv + 1), cols]
            return jnp.concatenate([vt, jnp.ones((16, vt.shape[1]), BF16)], axis=0)
        return get

    def latent_tile():
        kc, vc = k_cols(kc_ref), v_rows(vc_ref)
        if window:
            band = tq + 2 * WINDOW
            start = pl.multiple_of(jnp.clip(t * tq - WINDOW, 0, SEQ - band), LANES)
            kb = k_cols(kx_ref, pl.ds(start, band))
            vb = v_rows(vx_ref, pl.ds(start, band))
            kpos = start + lax.broadcasted_iota(jnp.int32, (band, tq), 0)
            qpos = t * tq + lax.broadcasted_iota(jnp.int32, (band, tq), 1)
            bias = jnp.where(jnp.abs(qpos - kpos) <= WINDOW, 0.0, NEG_BIG).astype(BF16)
            all_heads([kb, kc], [vb, vc], [bias, None])
        else:
            all_heads([k_cols(kx_ref), kc], [v_rows(vx_ref), vc], [None, None])

    def context_tile():
        all_heads([k_cols(kc_ref)], [v_rows(vc_ref)], [None])

    if ctx_tiles:
        pl.when(t < SEQ // tq)(latent_tile)
        pl.when(t >= SEQ // tq)(context_tile)
    else:
        latent_tile()


def _attn_call(q, k, v, sink, *, n_heads, window, ctx_tiles):
    tq = ATT_TQ
    n_xt = SEQ // tq
    n_ct = CTX_LEN // tq
    width = n_heads * HEAD_DIM
    has_sink = sink is not None

    def q_map(b, t):
        return (jnp.where(t < n_xt, b * n_xt + t, NX // tq + b * n_ct + (t - n_xt)), 0)

    in_specs = [
        pl.BlockSpec((tq, width), q_map),
        pl.BlockSpec((SEQ, 2 * LANES), lambda b, t: (b, 0)),
        pl.BlockSpec((LANES, SEQ), lambda b, t: (0, b)),
        pl.BlockSpec((CTX_LEN, 2 * LANES), lambda b, t: (NX // CTX_LEN + b, 0)),
        pl.BlockSpec((LANES, CTX_LEN), lambda b, t: (0, NX // CTX_LEN + b)),
    ]
    args = [q, k, v, k, v]
    if has_sink:
        in_specs.append(pl.BlockSpec(memory_space=pltpu.SMEM))
        args.append(sink)
    rows = NTOK if ctx_tiles else NX
    return pl.pallas_call(
        functools.partial(_attn_kernel, n_heads=n_heads, window=window, has_sink=has_sink,
                          ctx_tiles=ctx_tiles),
        grid=(BATCH, n_xt + (n_ct if ctx_tiles else 0)),
        in_specs=in_specs,
        out_specs=pl.BlockSpec((tq, width), q_map),
        out_shape=jax.ShapeDtypeStruct((rows, width), BF16),
        compiler_params=_cparams("parallel", "arbitrary"),
        name="win_attn" if window else "dense_attn",
    )(*args)


GLA_N_CTX = CTX_LEN // GLA_CHUNK
GLA_N_X = SEQ // GLA_CHUNK
GLA_STEPS = GLA_N_CTX + GLA_N_X
GLA_SAFE_LOG_DECAY = 60.0


def _gla_pos_block(b, pos):
    return jnp.where(pos < GLA_N_CTX, NX // GLA_CHUNK + b * GLA_N_CTX + pos,
                     b * GLA_N_X + pos - GLA_N_CTX)


def _gla_pos(t, fwd):
    if fwd:
        return t
    return jnp.where(t < GLA_N_CTX, GLA_N_CTX - 1 - t, GLA_STEPS + GLA_N_CTX - 1 - t)


def _bf16_pieces(x, n):
    pieces = []
    for _ in range(n):
        p = x.astype(BF16)
        pieces.append(p)
        x = x - p.astype(F32)
    return pieces


def _gla_intra_by_lag(q_ref, k_ref, v_ref, g_ref, tri, bmask_bf, oi_scr, fwd):
    C = GLA_CHUNK
    bc = sum(_dot(tri, piece) for piece in _bf16_pieces(g_ref[...], 3))
    b_pieces = _bf16_pieces(bc, 3)
    qf = q_ref[...].astype(F32)
    k, v = k_ref[...], v_ref[...]
    diff = (lax.broadcasted_iota(jnp.int32, (C, C), 0) - lax.broadcasted_iota(jnp.int32, (C, C), 1))
    if not fwd:
        diff = -diff

    def body(lag, acc):
        shift = (diff == lag).astype(F32).astype(BF16)
        bs = sum(_dot(shift, piece) for piece in b_pieces)
        w = qf * _dot(shift, k) * jnp.exp(jnp.minimum(bc - bs, 0.0))
        return acc + _dot(w.astype(BF16), bmask_bf) * _dot(shift, v)

    oi_scr[...] = lax.fori_loop(0, C, body, jnp.zeros((C, GLA_V_W), F32))


def _gla_chunk(q_ref, k_ref, v_ref, g_ref, tri, bmask_ref, hmask_ref, s_scr, oi_scr, slow, fwd):
    C = GLA_CHUNK
    bc = sum(_dot(tri, piece) for piece in _bf16_pieces(g_ref[...], 2))
    tot = bc[C - 1:C, :] if fwd else bc[0:1, :]
    qf = q_ref[...].astype(F32)
    kf = k_ref[...].astype(F32)
    v = v_ref[...]
    qb = (qf * jnp.exp(bc)).astype(BF16)
    s_old = s_scr[...]
    o = _dot(qb, s_old.astype(BF16))

    kp = (kf * jnp.exp(-bc)).astype(BF16)
    qst = jnp.concatenate([qb * hmask_ref[h] for h in range(GLA_HEADS)], axis=0)
    s = _dot_nt(qst, kp).astype(BF16)
    low = lax.broadcasted_iota(jnp.int32, (C, LANES), 1) < GLA_DV
    parts = []
    for p in range(GLA_HEADS // 2):
        a = jnp.concatenate([s[2 * p * C:(2 * p + 1) * C] * tri, s[(2 * p + 1) * C:(2 * p + 2) * C] * tri],
                            axis=0)
        r = _dot(a, v[:, LANES * p:LANES * (p + 1)])
        parts.append(jnp.where(low, r[:C], r[C:]))
    o_intra = jnp.concatenate(parts, axis=1)
    o = o + jnp.where(slow, oi_scr[...], o_intra)

    kt = (kf * jnp.exp(tot - bc)).T.astype(BF16)
    kv = _dot(kt, v)
    dcol = jnp.exp(jnp.broadcast_to(tot, (LANES, GLA_QK_W))).T
    dmat = jnp.concatenate([dcol] * (GLA_V_W // LANES), axis=1)
    s_scr[...] = dmat * s_old + bmask_ref[...] * kv
    return o


def _gla_kernel(qf_ref, kf_ref, vf_ref, gf_ref, qb_ref, kb_ref, vb_ref, gb_ref, gate_ref, gn_ref,
                bmask_ref, hmask_ref, bd_ref, trif_ref, trib_ref,
                o_ref, sf_scr, sb_scr, of_scr, ob_scr, oif_scr, oib_scr):
    t = pl.program_id(1)

    @pl.when(t == 0)
    def _():
        sf_scr[...] = jnp.zeros_like(sf_scr)
        sb_scr[...] = jnp.zeros_like(sb_scr)
        oif_scr[...] = jnp.zeros_like(oif_scr)
        oib_scr[...] = jnp.zeros_like(oib_scr)

    @pl.when(t < GLA_STEPS)
    def _():
        slow_f = jnp.min(jnp.sum(gf_ref[...], axis=0, keepdims=True)) < -GLA_SAFE_LOG_DECAY
        slow_b = jnp.min(jnp.sum(gb_ref[...], axis=0, keepdims=True)) < -GLA_SAFE_LOG_DECAY
        bmask_bf = bmask_ref[...].astype(BF16)

        @pl.when(slow_f)
        def _():
            _gla_intra_by_lag(qf_ref, kf_ref, vf_ref, gf_ref, trif_ref[...], bmask_bf, oif_scr, True)

        @pl.when(slow_b)
        def _():
            _gla_intra_by_lag(qb_ref, kb_ref, vb_ref, gb_ref, trib_ref[...], bmask_bf, oib_scr, False)

        of_scr[_gla_pos(t, True)] = _gla_chunk(
            qf_ref, kf_ref, vf_ref, gf_ref, trif_ref[...], bmask_ref, hmask_ref, sf_scr, oif_scr,
            slow_f, True)
        ob_scr[_gla_pos(t, False)] = _gla_chunk(
            qb_ref, kb_ref, vb_ref, gb_ref, trib_ref[...], bmask_ref, hmask_ref, sb_scr, oib_scr,
            slow_b, False)

    @pl.when(t >= GLA_STEPS)
    def _():
        pos = t - GLA_STEPS
        ot = of_scr[pos] + ob_scr[pos]
        ms = _dot((ot * ot).astype(BF16), bd_ref[...])
        y = ot * lax.rsqrt(ms + EPS) * gn_ref[...]
        gt = gate_ref[...].astype(F32)
        o_ref[...] = (y * (gt / (1.0 + jnp.exp(-gt)))).astype(BF16)


def _gla_call(qg, kg, vg, gg, dec, gn512, bmask, hmask, bd512, trif, trib):
    C = GLA_CHUNK
    last = GLA_STEPS - 1

    def scan_blk(fwd):
        return lambda b, t: (_gla_pos_block(b, _gla_pos(jnp.minimum(t, last), fwd)), 0)

    def dec_blk(fwd):
        return lambda b, t: (0 if fwd else 1,
                             _gla_pos_block(b, _gla_pos(jnp.minimum(t, last), fwd)), 0)

    def out_blk(b, t):
        return (_gla_pos_block(b, jnp.maximum(t - GLA_STEPS, 0)), 0)

    def chain_specs(fwd):
        return [pl.BlockSpec((C, GLA_QK_W), scan_blk(fwd)), pl.BlockSpec((C, GLA_QK_W), scan_blk(fwd)),
                pl.BlockSpec((C, GLA_V_W), scan_blk(fwd)),
                pl.BlockSpec((None, C, GLA_QK_W), dec_blk(fwd))]

    def const(shape):
        zeros = (0,) * len(shape)
        return pl.BlockSpec(shape, lambda b, t: zeros)

    return pl.pallas_call(
        _gla_kernel,
        grid=(BATCH, 2 * GLA_STEPS),
        in_specs=chain_specs(True) + chain_specs(False) + [
            pl.BlockSpec((C, GLA_V_W), out_blk),
            const((1, GLA_V_W)), const(bmask.shape), const(hmask.shape), const(bd512.shape),
            const((C, C)), const((C, C)),
        ],
        out_specs=pl.BlockSpec((C, GLA_V_W), out_blk),
        out_shape=jax.ShapeDtypeStruct((NTOK, GLA_V_W), BF16),
        scratch_shapes=[pltpu.VMEM((GLA_QK_W, GLA_V_W), F32), pltpu.VMEM((GLA_QK_W, GLA_V_W), F32),
                        pltpu.VMEM((GLA_STEPS, C, GLA_V_W), F32),
                        pltpu.VMEM((GLA_STEPS, C, GLA_V_W), F32),
                        pltpu.VMEM((C, GLA_V_W), F32), pltpu.VMEM((C, GLA_V_W), F32)],
        compiler_params=_cparams("parallel", "arbitrary"),
        name="gla_scan",
    )(qg, kg, vg, dec, qg, kg, vg, dec, gg, gn512, bmask, hmask, bd512, trif, trib)


def _mlp_kernel(*refs, final, split):
    x_ref, xc_ref = (refs[0], refs[1]) if split else (refs[0], None)
    (ma_ref, mb_ref, wa_ref, wb_ref, g1_ref, shift_ref, scale_ref, n2_ref,
     w1_ref, w3_ref, w2_ref, g2_ref, fg_ref, o_ref) = refs[2 if split else 1:]
    hc = FFN_HIDDEN // MLP_HIDDEN_CHUNKS
    ox = _dot(ma_ref[...], wa_ref[...]) + _dot(mb_ref[...], wb_ref[...])
    x1 = _residual_rows(x_ref, xc_ref, slice(None)) + g1_ref[...] * ox
    h = _norm_mod(x1, n2_ref, shift_ref, scale_ref)
    y = None
    for j in range(MLP_HIDDEN_CHUNKS):
        cols = slice(j * hc, (j + 1) * hc)
        a = _dot(h, w1_ref[:, cols])
        u = ((a / (1.0 + jnp.exp(-a))) * _dot(h, w3_ref[:, cols])).astype(BF16)
        part = _dot(u, w2_ref[cols, :])
        y = part if y is None else y + part
    x2 = x1 + g2_ref[...] * y
    if final:
        ms = jnp.mean(x2 * x2, axis=-1, keepdims=True)
        x2 = x2 * lax.rsqrt(ms + EPS) * fg_ref[...]
    o_ref[...] = x2


def _resident_spec(shape):
    zeros = (0,) * len(shape)
    return pl.BlockSpec(shape, lambda i: zeros, pipeline_mode=pl.Buffered(1))


def _mlp_call(mix_a, mix_b, col_b, w_a, w_b, streams, mod4, layer, norm_g, w1, w3, w2, final_g,
              n_tiles, final):
    half = D_MODEL // 2
    rows = n_tiles * ROW_TILE
    return pl.pallas_call(
        functools.partial(_mlp_kernel, final=final, split=len(streams) == 2),
        grid=(n_tiles,),
        in_specs=_stream_specs(streams) + [
            pl.BlockSpec((ROW_TILE, half), lambda i: (i, 0)),
            pl.BlockSpec((ROW_TILE, half), lambda i: (i, col_b)),
            _resident_spec((half, D_MODEL)), _resident_spec((half, D_MODEL)),
            _mod_spec(layer, 2), _mod_spec(layer, 3), _mod_spec(layer, 4),
            _full_spec((1, D_MODEL)),
            _resident_spec(w1.shape), _resident_spec(w3.shape), _resident_spec(w2.shape),
            _mod_spec(layer, 5), _full_spec((1, D_MODEL)),
        ],
        out_specs=_row_spec(D_MODEL),
        out_shape=jax.ShapeDtypeStruct((rows, D_MODEL), F32),
        compiler_params=_cparams("parallel"),
        name="out_proj_ffn",
    )(*streams, mix_a, mix_b, w_a, w_b, mod4, mod4, mod4, norm_g, w1, w3, w2, mod4, final_g)


def _rope_tables():
    pos = np.arange(SEQ)
    half = HEAD_DIM // 2
    inv = ROPE_THETA ** (-jnp.arange(0, half, 2, dtype=F32) / half)
    ang_r = (pos // GRID_W).astype(np.float32)[:, None] * inv[None]
    ang_c = (pos % GRID_W).astype(np.float32)[:, None] * inv[None]
    ang = jnp.concatenate([ang_r, ang_r, ang_c, ang_c] * 2, axis=-1)
    first = (np.arange(LANES) % 32) < 16
    cos, sin = jnp.cos(ang), jnp.sin(ang)
    sin_a = jnp.where(first[None], -sin, 0.0)
    sin_b = jnp.where(first[None], 0.0, sin)
    ident = jnp.ones((IN_ROW_TILE, LANES), F32)
    zero = jnp.zeros((IN_ROW_TILE, LANES), F32)
    return (jnp.concatenate([cos, ident]), jnp.concatenate([sin_a, zero]),
            jnp.concatenate([sin_b, zero]))


def _block_mean_matrix(width):
    idx = np.arange(width) // HEAD_DIM
    return jnp.asarray((idx[:, None] == idx[None, :]).astype(np.float32) / HEAD_DIM, BF16)


def kernel(x, c, ctx, c_ctx, ada_w, ada_b, norm1_g, norm2_g, ffn_w1, ffn_w3, ffn_w2,
           mixab_w_in, mixab_w_out, gla_gk_w, gla_gk_b, gla_norm_g, attn_qnorm_g, attn_knorm_g,
           win_w_in, win_w_out, win_sink, final_g):
    streams = (x.reshape(NX, D_MODEL), ctx.reshape(NC, D_MODEL))
    cvec = jnp.concatenate([c, c_ctx[None], jnp.zeros((MOD_ROWS - BATCH - 1, D_MODEL), F32)], axis=0)
    mod4 = _ada_call(cvec, ada_w, ada_b).reshape(DEPTH, MOD_ROWS, 1, 6 * D_MODEL)

    rope = _rope_tables()
    bd256 = _block_mean_matrix(2 * LANES)
    bd512 = _block_mean_matrix(GLA_V_W)
    head_of_k = np.arange(GLA_QK_W) // GLA_DK
    head_of_v = np.arange(GLA_V_W) // GLA_DV
    bmask = jnp.asarray((head_of_k[:, None] == head_of_v[None, :]).astype(np.float32))
    hmask = jnp.asarray((np.arange(GLA_HEADS)[:, None] == head_of_k[None, :]).astype(np.float32),
                        BF16).reshape(GLA_HEADS, 1, GLA_QK_W)
    tri_np = np.tril(np.ones((GLA_CHUNK, GLA_CHUNK), np.float32))
    trif, trib = jnp.asarray(tri_np, BF16), jnp.asarray(tri_np.T, BF16)
    final_row = final_g.reshape(1, D_MODEL)

    for l in range(DEPTH):
        with_ctx = l < DEPTH - 1
        i = l // 2
        n1 = norm1_g[l].reshape(1, D_MODEL)
        n2 = norm2_g[l].reshape(1, D_MODEL)
        if l % 2 == 0:
            w = mixab_w_in[i]
            w_lr = jnp.pad(w[:, 1536:1568], ((0, 0), (0, LANES - 2 * GLA_RANK)))
            w_main = jnp.concatenate([w[:, :1536], w[:, 1568:], w_lr], axis=1).astype(BF16)
            w_gk = jnp.zeros((LANES, 2 * GLA_QK_W), F32)
            w_gk = w_gk.at[:GLA_RANK, :GLA_QK_W].set(gla_gk_w[i, 0])
            w_gk = w_gk.at[GLA_RANK:2 * GLA_RANK, GLA_QK_W:].set(gla_gk_w[i, 1]).astype(BF16)
            gk_bias = gla_gk_b[i].reshape(1, 2 * GLA_QK_W)
            qn = jnp.tile(attn_qnorm_g[i], 2).reshape(1, LANES)
            kn = jnp.tile(attn_knorm_g[i], 2).reshape(1, LANES)
            qg, kg, vg, gg, dec, bq, bk, bv = _even_in_call(
                streams, mod4, l, n1, w_main, w_gk, gk_bias, qn, kn, bd256, rope)
            gn512 = jnp.tile(gla_norm_g[i], GLA_HEADS).reshape(1, GLA_V_W)
            mix_a = _gla_call(qg, kg, vg, gg, dec, gn512, bmask, hmask, bd512, trif, trib)
            mix_b = _attn_call(bq, bk, bv, None, n_heads=B_HEADS, window=False, ctx_tiles=True)
            w_out = mixab_w_out[i]
            w_a = w_out[:GLA_V_W].astype(BF16)
            w_b = w_out[GLA_V_W:].astype(BF16)
            col_b = 0
        else:
            q, k, v = _odd_in_call(streams[0], mod4, l, n1, win_w_in[i].astype(BF16), rope)
            mix_a = _attn_call(q, k, v, win_sink[i], n_heads=C_HEADS, window=True,
                               ctx_tiles=with_ctx)
            mix_b = mix_a
            w_out = win_w_out[i].astype(BF16)
            w_a, w_b = w_out[:D_MODEL // 2], w_out[D_MODEL // 2:]
            col_b = 1
        n_tiles = N_ALL_TILES if with_ctx else N_X_TILES
        streams = (_mlp_call(mix_a, mix_b, col_b, w_a, w_b, streams, mod4, l, n2,
                             ffn_w1[l].astype(BF16), ffn_w3[l].astype(BF16), ffn_w2[l].astype(BF16),
                             final_row, n_tiles, final=not with_ctx),)
    return streams[0].reshape(BATCH, SEQ, D_MODEL)
```

```python
import functools

import numpy as np
import jax
import jax.numpy as jnp
from jax import lax
from jax.experimental import pallas as pl
from jax.experimental.pallas import tpu as pltpu

F32 = jnp.float32
BF16 = jnp.bfloat16

D_MODEL = 1024
BATCH = 8
SEQ = 2048
DEPTH = 4
GRID_W = 64
CTX_LEN = 256
HEAD_DIM = 64
ROPE_THETA = 10000.0
EPS = 1e-6
WINDOW = 128
GLA_HEADS = 8
GLA_DK = 32
GLA_DV = 64
GLA_RANK = 16
GLA_TAU = 16.0
GLA_QK_W = GLA_HEADS * GLA_DK
GLA_V_W = GLA_HEADS * GLA_DV
B_HEADS = 8
C_HEADS = 16
FFN_HIDDEN = 2816

NX = BATCH * SEQ
NC = BATCH * CTX_LEN
NTOK = NX + NC

LANES = 128
ROW_TILE = 512
MLP_BIG_TILE = 1024
IN_ROW_TILE = 1024
IN_SUBTILES = 4
IN_SUB_ROWS = IN_ROW_TILE // IN_SUBTILES
ATT_SCORES_AHEAD = {False: 3, True: 4}
ATT_TQ = 256
GLA_CHUNK = 256
MLP_HIDDEN_CHUNKS = 11
VMEM_LIMIT = 52 * 1024 * 1024
NEG_BIG = -1e30
LOG2E = 1.4426950408889634
Q_SCALE = HEAD_DIM ** -0.5 * LOG2E

CTX_MOD_ROW = BATCH
MOD_ROWS = 16


def _cparams(*sem):
    return pltpu.CompilerParams(dimension_semantics=sem, vmem_limit_bytes=VMEM_LIMIT)


def _dot(a, b):
    return jnp.dot(a, b, preferred_element_type=F32)


def _dot_nt(a, b):
    return lax.dot_general(a, b, (((1,), (1,)), ((), ())), preferred_element_type=F32)


def _mod_spec(layer, chunk, tile=ROW_TILE):
    def row(i):
        return jnp.where(i < NX // tile, i // (SEQ // tile), CTX_MOD_ROW)
    return pl.BlockSpec((None, None, 1, D_MODEL), lambda i: (layer, row(i), 0, chunk))


def _full_spec(shape):
    zeros = (0,) * len(shape)
    return pl.BlockSpec(shape, lambda i: zeros)


def _ada_kernel(s_ref, w_ref, b_ref, o_ref):
    s = s_ref[...]
    s = (s / (1.0 + jnp.exp(-s))).astype(BF16)
    o_ref[...] = _dot(s, w_ref[...].astype(BF16)) + b_ref[...]


def _ada_call(cvec, ada_w, ada_b):
    tn = 1536
    return pl.pallas_call(
        _ada_kernel,
        grid=(DEPTH, 6 * D_MODEL // tn),
        in_specs=[
            pl.BlockSpec((MOD_ROWS, D_MODEL), lambda l, j: (0, 0)),
            pl.BlockSpec((None, D_MODEL, tn), lambda l, j: (l, 0, j)),
            pl.BlockSpec((None, 1, tn), lambda l, j: (l, 0, j)),
        ],
        out_specs=pl.BlockSpec((None, MOD_ROWS, tn), lambda l, j: (l, 0, j)),
        out_shape=jax.ShapeDtypeStruct((DEPTH, MOD_ROWS, 6 * D_MODEL), F32),
        compiler_params=_cparams("parallel", "parallel"),
        name="ada_mod",
    )(cvec, ada_w, ada_b.reshape(DEPTH, 1, 6 * D_MODEL))


def _residual_rows(x_ref, xc_ref, rows, tile=ROW_TILE):
    if xc_ref is None:
        return x_ref[rows, :]
    return jnp.where(pl.program_id(0) < NX // tile, x_ref[rows, :], xc_ref[rows, :])


def _kv_pair_layout(a, low):
    r = pltpu.roll(a, HEAD_DIM, 1)
    return jnp.where(low, a, r), jnp.where(low, r, a)


def _store_kv(k_ref, vt_ref, rows, k, v):
    low = lax.broadcasted_iota(jnp.int32, k.shape, 1) < HEAD_DIM
    k0, k1 = _kv_pair_layout(k, low)
    k_ref[rows, 0:LANES] = k0.astype(BF16)
    k_ref[rows, LANES:2 * LANES] = k1.astype(BF16)
    vt_ref[:, rows] = v.T.astype(BF16)


def _norm_mod(x, g_ref, shift_ref, scale_ref):
    ms = jnp.mean(x * x, axis=-1, keepdims=True)
    y = x * lax.rsqrt(ms + EPS) * g_ref[...]
    return (y * (1.0 + scale_ref[...]) + shift_ref[...]).astype(BF16)


def _rope(a, cos, sin_a, sin_b):
    return (a * cos + pltpu.roll(a, LANES - 16, 1) * sin_a + pltpu.roll(a, 16, 1) * sin_b)


def _even_in_kernel(*refs, split):
    x_ref, xc_ref = (refs[0], refs[1]) if split else (refs[0], None)
    (shift_ref, scale_ref, g_ref, w_ref, wg_ref, gb_ref, qn_ref, kn_ref, bd_ref, cos_ref, sa_ref,
     sb_ref, qg_ref, kg_ref, vg_ref, gg_ref, dec_ref, bq_ref, bk_ref, bv_ref) = refs[2 if split else 1:]
    for r in range(IN_SUBTILES):
        rows = pl.ds(r * IN_SUB_ROWS, IN_SUB_ROWS)
        h = _norm_mod(_residual_rows(x_ref, xc_ref, rows, IN_ROW_TILE), g_ref, shift_ref, scale_ref)

        def proj(lo, hi):
            return _dot(h, w_ref[:, lo:hi])

        qg_ref[rows, :] = (proj(0, 256) * GLA_DK ** -0.5).astype(BF16)
        kg_ref[rows, :] = proj(256, 512).astype(BF16)
        vg_ref[rows, :] = proj(512, 1024).astype(BF16)
        gg_ref[rows, :] = proj(1024, 1536).astype(BF16)

        a = proj(1536, 2432)

        lr = a[:, 768:896].astype(BF16)
        z = _dot(lr, wg_ref[...]) + gb_ref[...]
        logsig = jnp.minimum(z, 0.0) - jnp.log(1.0 + jnp.exp(-jnp.abs(z)))
        dec = logsig * (1.0 / GLA_TAU)
        dec_ref[0, rows, :] = dec[:, :GLA_QK_W]
        dec_ref[1, rows, :] = dec[:, GLA_QK_W:]

        sq = (a[:, :640] * a[:, :640]).astype(BF16)
        bd = bd_ref[...]
        ms = [_dot(sq[:, 0:256], bd), _dot(sq[:, 256:512], bd),
              _dot(sq[:, 512:640], bd[:LANES, :LANES])]
        cos, sa, sb = cos_ref[rows, :], sa_ref[rows, :], sb_ref[rows, :]
        for p in range(5):
            g_row = qn_ref[...] if p < 4 else kn_ref[...]
            msp = ms[p // 2][:, LANES * (p % 2):LANES * (p % 2 + 1)]
            n = a[:, LANES * p:LANES * (p + 1)] * lax.rsqrt(msp + EPS) * g_row
            rp = _rope(n, cos, sa, sb)
            if p < 4:
                bq_ref[rows, LANES * p:LANES * (p + 1)] = (rp * Q_SCALE).astype(BF16)
            else:
                _store_kv(bk_ref, bv_ref, rows, rp, a[:, 640:768])


def _odd_in_kernel(x_ref, shift_ref, scale_ref, g_ref, w_ref, cos_ref, sa_ref, sb_ref,
                   q_ref, k_ref, v_ref):
    for r in range(IN_SUBTILES):
        rows = pl.ds(r * IN_SUB_ROWS, IN_SUB_ROWS)
        h = _norm_mod(x_ref[rows, :], g_ref, shift_ref, scale_ref)
        cos, sa, sb = cos_ref[rows, :], sa_ref[rows, :], sb_ref[rows, :]
        a = _dot(h, w_ref[...])
        for p in range(C_HEADS // 2):
            rp = _rope(a[:, LANES * p:LANES * (p + 1)], cos, sa, sb)
            q_ref[rows, LANES * p:LANES * (p + 1)] = (rp * Q_SCALE).astype(BF16)
        _store_kv(k_ref, v_ref, rows, _rope(a[:, 1024:1152], cos, sa, sb), a[:, 1152:1280])


def _rope_spec():
    per = SEQ // IN_ROW_TILE
    return pl.BlockSpec((IN_ROW_TILE, LANES),
                        lambda i: (jnp.where(i < NX // IN_ROW_TILE, i % per, per), 0))


def _vt_spec():
    return pl.BlockSpec((LANES, IN_ROW_TILE), lambda i: (0, i))


def _row_spec(width, tile=ROW_TILE):
    return pl.BlockSpec((tile, width), lambda i: (i, 0))


def _stream_specs(streams, tile=ROW_TILE):
    if len(streams) == 1:
        return [_row_spec(D_MODEL, tile)]
    n_x = NX // tile
    return [pl.BlockSpec((tile, D_MODEL), lambda i: (jnp.minimum(i, n_x - 1), 0)),
            pl.BlockSpec((tile, D_MODEL), lambda i: (jnp.maximum(i - n_x, 0), 0))]


def _even_in_call(streams, mod4, layer, norm_g, w_main, w_gk, gk_bias, qn, kn, bd256, rope):
    outs = [(GLA_QK_W, BF16), (GLA_QK_W, BF16), (GLA_V_W, BF16), (GLA_V_W, BF16)]
    out_shape = [jax.ShapeDtypeStruct((NTOK, w), dt) for w, dt in outs]
    out_specs = [_row_spec(w, IN_ROW_TILE) for w, _ in outs]
    out_shape.append(jax.ShapeDtypeStruct((2, NTOK, GLA_QK_W), F32))
    out_specs.append(pl.BlockSpec((2, IN_ROW_TILE, GLA_QK_W), lambda i: (0, i, 0)))
    for w in (512, 2 * LANES):
        out_shape.append(jax.ShapeDtypeStruct((NTOK, w), BF16))
        out_specs.append(_row_spec(w, IN_ROW_TILE))
    out_shape.append(jax.ShapeDtypeStruct((LANES, NTOK), BF16))
    out_specs.append(_vt_spec())
    return pl.pallas_call(
        functools.partial(_even_in_kernel, split=len(streams) == 2),
        grid=(NTOK // IN_ROW_TILE,),
        in_specs=_stream_specs(streams, IN_ROW_TILE) + [
            _mod_spec(layer, 0, IN_ROW_TILE), _mod_spec(layer, 1, IN_ROW_TILE),
            _full_spec((1, D_MODEL)), _full_spec(w_main.shape),
            _full_spec(w_gk.shape), _full_spec(gk_bias.shape),
            _full_spec((1, LANES)), _full_spec((1, LANES)), _full_spec((2 * LANES, 2 * LANES)),
            _rope_spec(), _rope_spec(), _rope_spec(),
        ],
        out_specs=out_specs,
        out_shape=out_shape,
        compiler_params=_cparams("parallel"),
        name="even_in_proj",
    )(*streams, mod4, mod4, norm_g, w_main, w_gk, gk_bias, qn, kn, bd256, *rope)


def _odd_in_call(xs, mod4, layer, norm_g, w_in, rope):
    widths = (C_HEADS * HEAD_DIM, 2 * LANES)
    return pl.pallas_call(
        _odd_in_kernel,
        grid=(NTOK // IN_ROW_TILE,),
        in_specs=[
            _row_spec(D_MODEL, IN_ROW_TILE), _mod_spec(layer, 0, IN_ROW_TILE),
            _mod_spec(layer, 1, IN_ROW_TILE),
            _full_spec((1, D_MODEL)), _full_spec(w_in.shape),
            _rope_spec(), _rope_spec(), _rope_spec(),
        ],
        out_specs=[_row_spec(w, IN_ROW_TILE) for w in widths] + [_vt_spec()],
        out_shape=[jax.ShapeDtypeStruct((NTOK, w), BF16) for w in widths]
        + [jax.ShapeDtypeStruct((LANES, NTOK), BF16)],
        compiler_params=_cparams("parallel"),
        name="odd_in_proj",
    )(xs, mod4, mod4, norm_g, w_in, *rope)


def _attn_kernel(*refs, n_heads, window, has_sink, ctx_tiles):
    if has_sink:
        q_ref, kx_ref, vx_ref, kc_ref, vc_ref, sink_ref, o_ref = refs
    else:
        q_ref, kx_ref, vx_ref, kc_ref, vc_ref, o_ref = refs
        sink_ref = None
    tq = ATT_TQ
    t = pl.program_id(1)
    low = lax.broadcasted_iota(jnp.int32, (tq, LANES), 1) < HEAD_DIM
    zero = jnp.zeros((tq, LANES), BF16)

    def scores(hd, keys, biases):
        p, half = hd // 2, hd % 2
        kv = hd // (n_heads // 2)
        q = q_ref[:, LANES * p:LANES * (p + 1)]
        qs = jnp.where(low, q, zero) if half == 0 else jnp.where(low, zero, q)
        return [_dot_nt(k(kv), qs).astype(BF16) if b is None else _dot_nt(k(kv), qs).astype(BF16) + b
                for k, b in zip(keys, biases)]

    def finish(hd, ss, values):
        kv = hd // (n_heads // 2)
        m = functools.reduce(jnp.maximum, [jnp.max(x, axis=0, keepdims=True) for x in ss])
        if has_sink:
            sk = (sink_ref[hd] * LOG2E).astype(BF16)
            m = jnp.maximum(m, sk)
        acc = None
        for x, v in zip(ss, values):
            part = _dot(v(kv), jnp.exp2(x - m))
            acc = part if acc is None else acc + part
        den = acc[HEAD_DIM:HEAD_DIM + 1, :]
        if has_sink:
            den = den + jnp.exp2(sk - m).astype(F32)
        return acc[:HEAD_DIM, :] / den

    def all_heads(keys, values, biases):
        ahead = ATT_SCORES_AHEAD[window]
        pending = [scores(hd, keys, biases) for hd in range(ahead)]
        outs = []
        for hd in range(n_heads):
            if hd + ahead < n_heads:
                pending.append(scores(hd + ahead, keys, biases))
            outs.append(finish(hd, pending.pop(0), values))
            if hd % 2 == 1:
                ot = jnp.concatenate(outs, axis=0)
                o_ref[:, LANES * (hd // 2):LANES * (hd // 2 + 1)] = ot.T.astype(BF16)
                outs = []

    def k_cols(ref, rows=slice(None)):
        return lambda kv: ref[rows, LANES * kv:LANES * (kv + 1)]

    def v_rows(ref, cols=slice(None)):
        def get(kv):
            vt = ref[HEAD_DIM * kv:HEAD_DIM * (kv + 1), cols]
            return jnp.concatenate([vt, jnp.ones((16, vt.shape[1]), BF16)], axis=0)
        return get

    def latent_tile():
        kc, vc = k_cols(kc_ref), v_rows(vc_ref)
        if window:
            band = tq + 2 * WINDOW
            start = pl.multiple_of(jnp.clip(t * tq - WINDOW, 0, SEQ - band), LANES)
            kb = k_cols(kx_ref, pl.ds(start, band))
            vb = v_rows(vx_ref, pl.ds(start, band))
            kpos = start + lax.broadcasted_iota(jnp.int32, (band, tq), 0)
            qpos = t * tq + lax.broadcasted_iota(jnp.int32, (band, tq), 1)
            bias = jnp.where(jnp.abs(qpos - kpos) <= WINDOW, 0.0, NEG_BIG).astype(BF16)
            all_heads([kb, kc], [vb, vc], [bias, None])
        else:
            all_heads([k_cols(kx_ref), kc], [v_rows(vx_ref), vc], [None, None])

    def context_tile():
        all_heads([k_cols(kc_ref)], [v_rows(vc_ref)], [None])

    if ctx_tiles:
        pl.when(t < SEQ // tq)(latent_tile)
        pl.when(t >= SEQ // tq)(context_tile)
    else:
        latent_tile()


def _attn_call(q, k, v, sink, *, n_heads, window, ctx_tiles):
    tq = ATT_TQ
    n_xt = SEQ // tq
    n_ct = CTX_LEN // tq
    width = n_heads * HEAD_DIM
    has_sink = sink is not None

    def q_map(b, t):
        return (jnp.where(t < n_xt, b * n_xt + t, NX // tq + b * n_ct + (t - n_xt)), 0)

    in_specs = [
        pl.BlockSpec((tq, width), q_map),
        pl.BlockSpec((SEQ, 2 * LANES), lambda b, t: (b, 0)),
        pl.BlockSpec((LANES, SEQ), lambda b, t: (0, b)),
        pl.BlockSpec((CTX_LEN, 2 * LANES), lambda b, t: (NX // CTX_LEN + b, 0)),
        pl.BlockSpec((LANES, CTX_LEN), lambda b, t: (0, NX // CTX_LEN + b)),
    ]
    args = [q, k, v, k, v]
    if has_sink:
        in_specs.append(pl.BlockSpec(memory_space=pltpu.SMEM))
        args.append(sink)
    rows = NTOK if ctx_tiles else NX
    return pl.pallas_call(
        functools.partial(_attn_kernel, n_heads=n_heads, window=window, has_sink=has_sink,
                          ctx_tiles=ctx_tiles),
        grid=(BATCH, n_xt + (n_ct if ctx_tiles else 0)),
        in_specs=in_specs,
        out_specs=pl.BlockSpec((tq, width), q_map),
        out_shape=jax.ShapeDtypeStruct((rows, width), BF16),
        compiler_params=_cparams("parallel", "arbitrary"),
        name="win_attn" if window else "dense_attn",
    )(*args)


GLA_N_CTX = CTX_LEN // GLA_CHUNK
GLA_N_X = SEQ // GLA_CHUNK
GLA_STEPS = GLA_N_CTX + GLA_N_X
GLA_SAFE_LOG_DECAY = 60.0


def _gla_pos_block(b, pos):
    return jnp.where(pos < GLA_N_CTX, NX // GLA_CHUNK + b * GLA_N_CTX + pos,
                     b * GLA_N_X + pos - GLA_N_CTX)


def _gla_pos(t, fwd):
    if fwd:
        return t
    return jnp.where(t < GLA_N_CTX, GLA_N_CTX - 1 - t, GLA_STEPS + GLA_N_CTX - 1 - t)


def _bf16_pieces(x, n):
    pieces = []
    for _ in range(n):
        p = x.astype(BF16)
        pieces.append(p)
        x = x - p.astype(F32)
    return pieces


def _gla_intra_by_lag(q_ref, k_ref, v_ref, g_ref, tri, bmask_bf, oi_scr, fwd):
    C = GLA_CHUNK
    bc = sum(_dot(tri, piece) for piece in _bf16_pieces(g_ref[...], 3))
    b_pieces = _bf16_pieces(bc, 3)
    qf = q_ref[...].astype(F32)
    k, v = k_ref[...], v_ref[...]
    diff = (lax.broadcasted_iota(jnp.int32, (C, C), 0) - lax.broadcasted_iota(jnp.int32, (C, C), 1))
    if not fwd:
        diff = -diff

    def body(lag, acc):
        shift = (diff == lag).astype(F32).astype(BF16)
        bs = sum(_dot(shift, piece) for piece in b_pieces)
        w = qf * _dot(shift, k) * jnp.exp(jnp.minimum(bc - bs, 0.0))
        return acc + _dot(w.astype(BF16), bmask_bf) * _dot(shift, v)

    oi_scr[...] = lax.fori_loop(0, C, body, jnp.zeros((C, GLA_V_W), F32))


def _interleave(*chains):
    results = [None] * len(chains)
    live = list(range(len(chains)))
    while live:
        for i in list(live):
            try:
                next(chains[i])
            except StopIteration as done:
                results[i] = done.value
                live.remove(i)
    return results


def _gla_chunk(q_ref, k_ref, v_ref, g_ref, tri, bmask_ref, hmask_ref, s_scr, oi_scr, slow, fwd):
    C = GLA_CHUNK
    bc = sum(_dot(tri, piece) for piece in _bf16_pieces(g_ref[...], 2))
    yield
    tot = bc[C - 1:C, :] if fwd else bc[0:1, :]
    qf = q_ref[...].astype(F32)
    kf = k_ref[...].astype(F32)
    v = v_ref[...]
    qb = (qf * jnp.exp(bc)).astype(BF16)
    s_old = s_scr[...]
    o = _dot(qb, s_old.astype(BF16))
    kp = (kf * jnp.exp(-bc)).astype(BF16)
    qst = jnp.concatenate([qb * hmask_ref[h] for h in range(GLA_HEADS)], axis=0)
    s = _dot_nt(qst, kp).astype(BF16)
    kt = (kf * jnp.exp(tot - bc)).T.astype(BF16)
    kv = _dot(kt, v)
    yield
    low = lax.broadcasted_iota(jnp.int32, (C, LANES), 1) < GLA_DV
    parts = []
    for p in range(GLA_HEADS // 2):
        a = jnp.concatenate([s[2 * p * C:(2 * p + 1) * C] * tri, s[(2 * p + 1) * C:(2 * p + 2) * C] * tri],
                            axis=0)
        r = _dot(a, v[:, LANES * p:LANES * (p + 1)])
        parts.append(jnp.where(low, r[:C], r[C:]))
    yield
    o_intra = jnp.concatenate(parts, axis=1)
    o = o + jnp.where(slow, oi_scr[...], o_intra)
    dcol = jnp.exp(jnp.broadcast_to(tot, (LANES, GLA_QK_W))).T
    dmat = jnp.concatenate([dcol] * (GLA_V_W // LANES), axis=1)
    s_scr[...] = dmat * s_old + bmask_ref[...] * kv
    return o


def _gla_kernel(qf_ref, kf_ref, vf_ref, gf_ref, qb_ref, kb_ref, vb_ref, gb_ref, gate_ref, gn_ref,
                bmask_ref, hmask_ref, bd_ref, trif_ref, trib_ref,
                o_ref, sf_scr, sb_scr, of_scr, ob_scr, oif_scr, oib_scr):
    b = pl.program_id(0)
    t = pl.program_id(1)

    @pl.when(jnp.logical_and(b < BATCH, t == 0))
    def _():
        sf_scr[...] = jnp.zeros_like(sf_scr)
        sb_scr[...] = jnp.zeros_like(sb_scr)
        oif_scr[...] = jnp.zeros_like(oif_scr)
        oib_scr[...] = jnp.zeros_like(oib_scr)

    @pl.when(b < BATCH)
    def _():
        slow_f = jnp.min(jnp.sum(gf_ref[...], axis=0, keepdims=True)) < -GLA_SAFE_LOG_DECAY
        slow_b = jnp.min(jnp.sum(gb_ref[...], axis=0, keepdims=True)) < -GLA_SAFE_LOG_DECAY
        bmask_bf = bmask_ref[...].astype(BF16)

        @pl.when(slow_f)
        def _():
            _gla_intra_by_lag(qf_ref, kf_ref, vf_ref, gf_ref, trif_ref[...], bmask_bf, oif_scr, True)

        @pl.when(slow_b)
        def _():
            _gla_intra_by_lag(qb_ref, kb_ref, vb_ref, gb_ref, trib_ref[...], bmask_bf, oib_scr, False)

        o_f, o_b = _interleave(
            _gla_chunk(qf_ref, kf_ref, vf_ref, gf_ref, trif_ref[...], bmask_ref, hmask_ref, sf_scr,
                       oif_scr, slow_f, True),
            _gla_chunk(qb_ref, kb_ref, vb_ref, gb_ref, trib_ref[...], bmask_ref, hmask_ref, sb_scr,
                       oib_scr, slow_b, False))
        of_scr[b % 2, _gla_pos(t, True)] = o_f
        ob_scr[b % 2, _gla_pos(t, False)] = o_b

    @pl.when(b >= 1)
    def _():
        ot = of_scr[(b + 1) % 2, t] + ob_scr[(b + 1) % 2, t]
        ms = _dot((ot * ot).astype(BF16), bd_ref[...])
        y = ot * lax.rsqrt(ms + EPS) * gn_ref[...]
        gt = gate_ref[...].astype(F32)
        o_ref[...] = (y * (gt / (1.0 + jnp.exp(-gt)))).astype(BF16)


def _gla_call(qg, kg, vg, gg, dec, gn512, bmask, hmask, bd512, trif, trib):
    C = GLA_CHUNK
    last = GLA_STEPS - 1

    def scan_row_blk(b, t, fwd):
        scanning = b < BATCH
        return _gla_pos_block(jnp.minimum(b, BATCH - 1), _gla_pos(jnp.where(scanning, t, last), fwd))

    def scan_blk(fwd):
        return lambda b, t: (scan_row_blk(b, t, fwd), 0)

    def dec_blk(fwd):
        return lambda b, t: (0 if fwd else 1, scan_row_blk(b, t, fwd), 0)

    def out_blk(b, t):
        return (_gla_pos_block(jnp.maximum(b - 1, 0), jnp.where(b >= 1, t, 0)), 0)

    def chain_specs(fwd):
        return [pl.BlockSpec((C, GLA_QK_W), scan_blk(fwd)), pl.BlockSpec((C, GLA_QK_W), scan_blk(fwd)),
                pl.BlockSpec((C, GLA_V_W), scan_blk(fwd)),
                pl.BlockSpec((None, C, GLA_QK_W), dec_blk(fwd))]

    def const(shape):
        zeros = (0,) * len(shape)
        return pl.BlockSpec(shape, lambda b, t: zeros)

    return pl.pallas_call(
        _gla_kernel,
        grid=(BATCH + 1, GLA_STEPS),
        in_specs=chain_specs(True) + chain_specs(False) + [
            pl.BlockSpec((C, GLA_V_W), out_blk),
            const((1, GLA_V_W)), const(bmask.shape), const(hmask.shape), const(bd512.shape),
            const((C, C)), const((C, C)),
        ],
        out_specs=pl.BlockSpec((C, GLA_V_W), out_blk),
        out_shape=jax.ShapeDtypeStruct((NTOK, GLA_V_W), BF16),
        scratch_shapes=[pltpu.VMEM((GLA_QK_W, GLA_V_W), F32), pltpu.VMEM((GLA_QK_W, GLA_V_W), F32),
                        pltpu.VMEM((2, GLA_STEPS, C, GLA_V_W), F32),
                        pltpu.VMEM((2, GLA_STEPS, C, GLA_V_W), F32),
                        pltpu.VMEM((C, GLA_V_W), F32), pltpu.VMEM((C, GLA_V_W), F32)],
        compiler_params=_cparams("arbitrary", "arbitrary"),
        name="gla_scan",
    )(qg, kg, vg, dec, qg, kg, vg, dec, gg, gn512, bmask, hmask, bd512, trif, trib)


def _mlp_kernel(*refs, final, split):
    x_ref, xc_ref = (refs[0], refs[1]) if split else (refs[0], None)
    (ma_ref, mb_ref, wa_ref, wb_ref, g1_ref, shift_ref, scale_ref, n2_ref,
     w1_ref, w3_ref, w2_ref, g2_ref, fg_ref, o_ref) = refs[2 if split else 1:]
    hc = FFN_HIDDEN // MLP_HIDDEN_CHUNKS
    ox = _dot(ma_ref[...], wa_ref[...]) + _dot(mb_ref[...], wb_ref[...])
    x1 = _residual_rows(x_ref, xc_ref, slice(None)) + g1_ref[...] * ox
    h = _norm_mod(x1, n2_ref, shift_ref, scale_ref)
    y = None
    for j in range(MLP_HIDDEN_CHUNKS):
        cols = slice(j * hc, (j + 1) * hc)
        a = _dot(h, w1_ref[:, cols])
        u = ((a / (1.0 + jnp.exp(-a))) * _dot(h, w3_ref[:, cols])).astype(BF16)
        part = _dot(u, w2_ref[cols, :])
        y = part if y is None else y + part
    x2 = x1 + g2_ref[...] * y
    if final:
        ms = jnp.mean(x2 * x2, axis=-1, keepdims=True)
        x2 = x2 * lax.rsqrt(ms + EPS) * fg_ref[...]
    o_ref[...] = x2


def _resident_spec(shape):
    zeros = (0,) * len(shape)
    return pl.BlockSpec(shape, lambda i: zeros, pipeline_mode=pl.Buffered(1))


def _mlp_call(mix_a, mix_b, col_b, w_a, w_b, streams, mod4, layer, norm_g, w1, w3, w2, final_g,
              rows, final):
    half = D_MODEL // 2
    tile = ROW_TILE if len(streams) == 2 else MLP_BIG_TILE
    return pl.pallas_call(
        functools.partial(_mlp_kernel, final=final, split=len(streams) == 2),
        grid=(rows // tile,),
        in_specs=_stream_specs(streams, tile) + [
            pl.BlockSpec((tile, half), lambda i: (i, 0)),
            pl.BlockSpec((tile, half), lambda i: (i, col_b)),
            _resident_spec((half, D_MODEL)), _resident_spec((half, D_MODEL)),
            _mod_spec(layer, 2, tile), _mod_spec(layer, 3, tile), _mod_spec(layer, 4, tile),
            _full_spec((1, D_MODEL)),
            _resident_spec(w1.shape), _resident_spec(w3.shape), _resident_spec(w2.shape),
            _mod_spec(layer, 5, tile), _full_spec((1, D_MODEL)),
        ],
        out_specs=_row_spec(D_MODEL, tile),
        out_shape=jax.ShapeDtypeStruct((rows, D_MODEL), F32),
        compiler_params=_cparams("parallel"),
        name="out_proj_ffn",
    )(*streams, mix_a, mix_b, w_a, w_b, mod4, mod4, mod4, norm_g, w1, w3, w2, mod4, final_g)


def _rope_tables():
    pos = np.arange(SEQ)
    half = HEAD_DIM // 2
    inv = ROPE_THETA ** (-jnp.arange(0, half, 2, dtype=F32) / half)
    ang_r = (pos // GRID_W).astype(np.float32)[:, None] * inv[None]
    ang_c = (pos % GRID_W).astype(np.float32)[:, None] * inv[None]
    ang = jnp.concatenate([ang_r, ang_r, ang_c, ang_c] * 2, axis=-1)
    first = (np.arange(LANES) % 32) < 16
    cos, sin = jnp.cos(ang), jnp.sin(ang)
    sin_a = jnp.where(first[None], -sin, 0.0)
    sin_b = jnp.where(first[None], 0.0, sin)
    ident = jnp.ones((IN_ROW_TILE, LANES), F32)
    zero = jnp.zeros((IN_ROW_TILE, LANES), F32)
    return (jnp.concatenate([cos, ident]), jnp.concatenate([sin_a, zero]),
            jnp.concatenate([sin_b, zero]))


def _block_mean_matrix(width):
    idx = np.arange(width) // HEAD_DIM
    return jnp.asarray((idx[:, None] == idx[None, :]).astype(np.float32) / HEAD_DIM, BF16)


def kernel(x, c, ctx, c_ctx, ada_w, ada_b, norm1_g, norm2_g, ffn_w1, ffn_w3, ffn_w2,
           mixab_w_in, mixab_w_out, gla_gk_w, gla_gk_b, gla_norm_g, attn_qnorm_g, attn_knorm_g,
           win_w_in, win_w_out, win_sink, final_g):
    streams = (x.reshape(NX, D_MODEL), ctx.reshape(NC, D_MODEL))
    cvec = jnp.concatenate([c, c_ctx[None], jnp.zeros((MOD_ROWS - BATCH - 1, D_MODEL), F32)], axis=0)
    mod4 = _ada_call(cvec, ada_w, ada_b).reshape(DEPTH, MOD_ROWS, 1, 6 * D_MODEL)

    rope = _rope_tables()
    bd256 = _block_mean_matrix(2 * LANES)
    bd512 = _block_mean_matrix(GLA_V_W)
    head_of_k = np.arange(GLA_QK_W) // GLA_DK
    head_of_v = np.arange(GLA_V_W) // GLA_DV
    bmask = jnp.asarray((head_of_k[:, None] == head_of_v[None, :]).astype(np.float32))
    hmask = jnp.asarray((np.arange(GLA_HEADS)[:, None] == head_of_k[None, :]).astype(np.float32),
                        BF16).reshape(GLA_HEADS, 1, GLA_QK_W)
    tri_np = np.tril(np.ones((GLA_CHUNK, GLA_CHUNK), np.float32))
    trif, trib = jnp.asarray(tri_np, BF16), jnp.asarray(tri_np.T, BF16)
    final_row = final_g.reshape(1, D_MODEL)

    for l in range(DEPTH):
        with_ctx = l < DEPTH - 1
        i = l // 2
        n1 = norm1_g[l].reshape(1, D_MODEL)
        n2 = norm2_g[l].reshape(1, D_MODEL)
        if l % 2 == 0:
            w = mixab_w_in[i]
            w_lr = jnp.pad(w[:, 1536:1568], ((0, 0), (0, LANES - 2 * GLA_RANK)))
            w_main = jnp.concatenate([w[:, :1536], w[:, 1568:], w_lr], axis=1).astype(BF16)
            w_gk = jnp.zeros((LANES, 2 * GLA_QK_W), F32)
            w_gk = w_gk.at[:GLA_RANK, :GLA_QK_W].set(gla_gk_w[i, 0])
            w_gk = w_gk.at[GLA_RANK:2 * GLA_RANK, GLA_QK_W:].set(gla_gk_w[i, 1]).astype(BF16)
            gk_bias = gla_gk_b[i].reshape(1, 2 * GLA_QK_W)
            qn = jnp.tile(attn_qnorm_g[i], 2).reshape(1, LANES)
            kn = jnp.tile(attn_knorm_g[i], 2).reshape(1, LANES)
            qg, kg, vg, gg, dec, bq, bk, bv = _even_in_call(
                streams, mod4, l, n1, w_main, w_gk, gk_bias, qn, kn, bd256, rope)
            gn512 = jnp.tile(gla_norm_g[i], GLA_HEADS).reshape(1, GLA_V_W)
            mix_a = _gla_call(qg, kg, vg, gg, dec, gn512, bmask, hmask, bd512, trif, trib)
            mix_b = _attn_call(bq, bk, bv, None, n_heads=B_HEADS, window=False, ctx_tiles=True)
            w_out = mixab_w_out[i]
            w_a = w_out[:GLA_V_W].astype(BF16)
            w_b = w_out[GLA_V_W:].astype(BF16)
            col_b = 0
        else:
            q, k, v = _odd_in_call(streams[0], mod4, l, n1, win_w_in[i].astype(BF16), rope)
            mix_a = _attn_call(q, k, v, win_sink[i], n_heads=C_HEADS, window=True,
                               ctx_tiles=with_ctx)
            mix_b = mix_a
            w_out = win_w_out[i].astype(BF16)
            w_a, w_b = w_out[:D_MODEL // 2], w_out[D_MODEL // 2:]
            col_b = 1
        streams = (_mlp_call(mix_a, mix_b, col_b, w_a, w_b, streams, mod4, l, n2,
                             ffn_w1[l].astype(BF16), ffn_w3[l].astype(BF16), ffn_w2[l].astype(BF16),
                             final_row, NTOK if with_ctx else NX, final=not with_ctx),)
    return streams[0].reshape(BATCH, SEQ, D_MODEL)
```

```python
import functools

import numpy as np
import jax
import jax.numpy as jnp
from jax import lax
from jax.experimental import pallas as pl
from jax.experimental.pallas import tpu as pltpu

F32 = jnp.float32
BF16 = jnp.bfloat16

D_MODEL = 1024
BATCH = 8
SEQ = 2048
DEPTH = 4
GRID_W = 64
CTX_LEN = 256
HEAD_DIM = 64
ROPE_THETA = 10000.0
EPS = 1e-6
WINDOW = 128
GLA_HEADS = 8
GLA_DK = 32
GLA_DV = 64
GLA_RANK = 16
GLA_TAU = 16.0
GLA_QK_W = GLA_HEADS * GLA_DK
GLA_V_W = GLA_HEADS * GLA_DV
B_HEADS = 8
C_HEADS = 16
FFN_HIDDEN = 2816

NX = BATCH * SEQ
NC = BATCH * CTX_LEN
NTOK = NX + NC

LANES = 128
ROW_TILE = 512
MLP_BIG_TILE = 1024
MLP_SUB_ROWS = 512
IN_ROW_TILE = 1024
IN_SUBTILES = 2
IN_SUB_ROWS = IN_ROW_TILE // IN_SUBTILES
ATT_SCORES_AHEAD = {False: 3, True: 4}
ATT_TQ = 256
GLA_CHUNK = 256
MLP_HIDDEN_CHUNKS = 11
VMEM_LIMIT = 52 * 1024 * 1024
NEG_BIG = -1e30
LOG2E = 1.4426950408889634
Q_SCALE = HEAD_DIM ** -0.5 * LOG2E

CTX_MOD_ROW = BATCH
MOD_ROWS = 16


def _cparams(*sem):
    return pltpu.CompilerParams(dimension_semantics=sem, vmem_limit_bytes=VMEM_LIMIT)


def _dot(a, b):
    return jnp.dot(a, b, preferred_element_type=F32)


def _dot_nt(a, b):
    return lax.dot_general(a, b, (((1,), (1,)), ((), ())), preferred_element_type=F32)


def _mod_spec(layer, chunk, tile=ROW_TILE):
    def row(i):
        return jnp.where(i < NX // tile, i // (SEQ // tile), CTX_MOD_ROW)
    return pl.BlockSpec((None, None, 1, D_MODEL), lambda i: (layer, row(i), 0, chunk))


def _full_spec(shape):
    zeros = (0,) * len(shape)
    return pl.BlockSpec(shape, lambda i: zeros)


def _ada_kernel(s_ref, w_ref, b_ref, o_ref):
    s = s_ref[...]
    s = (s / (1.0 + jnp.exp(-s))).astype(BF16)
    o_ref[...] = _dot(s, w_ref[...].astype(BF16)) + b_ref[...]


def _ada_call(cvec, ada_w, ada_b):
    tn = 1536
    return pl.pallas_call(
        _ada_kernel,
        grid=(DEPTH, 6 * D_MODEL // tn),
        in_specs=[
            pl.BlockSpec((MOD_ROWS, D_MODEL), lambda l, j: (0, 0)),
            pl.BlockSpec((None, D_MODEL, tn), lambda l, j: (l, 0, j)),
            pl.BlockSpec((None, 1, tn), lambda l, j: (l, 0, j)),
        ],
        out_specs=pl.BlockSpec((None, MOD_ROWS, tn), lambda l, j: (l, 0, j)),
        out_shape=jax.ShapeDtypeStruct((DEPTH, MOD_ROWS, 6 * D_MODEL), F32),
        compiler_params=_cparams("parallel", "parallel"),
        name="ada_mod",
    )(cvec, ada_w, ada_b.reshape(DEPTH, 1, 6 * D_MODEL))


def _residual_rows(x_ref, xc_ref, rows, tile=ROW_TILE):
    if xc_ref is None:
        return x_ref[rows, :]
    return jnp.where(pl.program_id(0) < NX // tile, x_ref[rows, :], xc_ref[rows, :])


def _kv_pair_layout(a, low):
    r = pltpu.roll(a, HEAD_DIM, 1)
    return jnp.where(low, a, r), jnp.where(low, r, a)


def _store_kv(k_ref, vt_ref, rows, k, v):
    low = lax.broadcasted_iota(jnp.int32, k.shape, 1) < HEAD_DIM
    k0, k1 = _kv_pair_layout(k, low)
    k_ref[rows, 0:LANES] = k0.astype(BF16)
    k_ref[rows, LANES:2 * LANES] = k1.astype(BF16)
    vt_ref[:, rows] = v.T.astype(BF16)


def _norm_mod(x, g_ref, shift_ref, scale_ref):
    ms = jnp.mean(x * x, axis=-1, keepdims=True)
    y = x * lax.rsqrt(ms + EPS) * g_ref[...]
    return (y * (1.0 + scale_ref[...]) + shift_ref[...]).astype(BF16)


def _rope(a, cos, sin_a, sin_b):
    return (a * cos + pltpu.roll(a, LANES - 16, 1) * sin_a + pltpu.roll(a, 16, 1) * sin_b)


def _even_in_kernel(*refs, split):
    x_ref, xc_ref = (refs[0], refs[1]) if split else (refs[0], None)
    (shift_ref, scale_ref, g_ref, w_ref, wg_ref, gb_ref, qn_ref, kn_ref, bd_ref, cos_ref, sa_ref,
     sb_ref, qg_ref, kg_ref, vg_ref, gg_ref, dec_ref, bq_ref, bk_ref, bv_ref) = refs[2 if split else 1:]
    for r in range(IN_SUBTILES):
        rows = pl.ds(r * IN_SUB_ROWS, IN_SUB_ROWS)
        h = _norm_mod(_residual_rows(x_ref, xc_ref, rows, IN_ROW_TILE), g_ref, shift_ref, scale_ref)

        def proj(lo, hi):
            return _dot(h, w_ref[:, lo:hi])

        qg_ref[rows, :] = (proj(0, 256) * GLA_DK ** -0.5).astype(BF16)
        kg_ref[rows, :] = proj(256, 512).astype(BF16)
        vg_ref[rows, :] = proj(512, 1024).astype(BF16)
        gg_ref[rows, :] = proj(1024, 1536).astype(BF16)

        a = proj(1536, 2432)

        lr = a[:, 768:896].astype(BF16)
        z = _dot(lr, wg_ref[...]) + gb_ref[...]
        logsig = jnp.minimum(z, 0.0) - jnp.log(1.0 + jnp.exp(-jnp.abs(z)))
        dec = logsig * (1.0 / GLA_TAU)
        dec_ref[0, rows, :] = dec[:, :GLA_QK_W]
        dec_ref[1, rows, :] = dec[:, GLA_QK_W:]

        sq = (a[:, :640] * a[:, :640]).astype(BF16)
        bd = bd_ref[...]
        ms = [_dot(sq[:, 0:256], bd), _dot(sq[:, 256:512], bd),
              _dot(sq[:, 512:640], bd[:LANES, :LANES])]
        cos, sa, sb = cos_ref[rows, :], sa_ref[rows, :], sb_ref[rows, :]
        for p in range(5):
            g_row = qn_ref[...] if p < 4 else kn_ref[...]
            msp = ms[p // 2][:, LANES * (p % 2):LANES * (p % 2 + 1)]
            n = a[:, LANES * p:LANES * (p + 1)] * lax.rsqrt(msp + EPS) * g_row
            rp = _rope(n, cos, sa, sb)
            if p < 4:
                bq_ref[rows, LANES * p:LANES * (p + 1)] = (rp * Q_SCALE).astype(BF16)
            else:
                _store_kv(bk_ref, bv_ref, rows, rp, a[:, 640:768])


def _odd_in_kernel(x_ref, shift_ref, scale_ref, g_ref, w_ref, cos_ref, sa_ref, sb_ref,
                   q_ref, k_ref, v_ref):
    for r in range(IN_SUBTILES):
        rows = pl.ds(r * IN_SUB_ROWS, IN_SUB_ROWS)
        h = _norm_mod(x_ref[rows, :], g_ref, shift_ref, scale_ref)
        cos, sa, sb = cos_ref[rows, :], sa_ref[rows, :], sb_ref[rows, :]
        a = _dot(h, w_ref[...])
        for p in range(C_HEADS // 2):
            rp = _rope(a[:, LANES * p:LANES * (p + 1)], cos, sa, sb)
            q_ref[rows, LANES * p:LANES * (p + 1)] = (rp * Q_SCALE).astype(BF16)
        _store_kv(k_ref, v_ref, rows, _rope(a[:, 1024:1152], cos, sa, sb), a[:, 1152:1280])


def _rope_spec():
    per = SEQ // IN_ROW_TILE
    return pl.BlockSpec((IN_ROW_TILE, LANES),
                        lambda i: (jnp.where(i < NX // IN_ROW_TILE, i % per, per), 0))


def _vt_spec():
    return pl.BlockSpec((LANES, IN_ROW_TILE), lambda i: (0, i))


def _row_spec(width, tile=ROW_TILE):
    return pl.BlockSpec((tile, width), lambda i: (i, 0))


def _stream_specs(streams, tile=ROW_TILE):
    if len(streams) == 1:
        return [_row_spec(D_MODEL, tile)]
    n_x = NX // tile
    return [pl.BlockSpec((tile, D_MODEL), lambda i: (jnp.minimum(i, n_x - 1), 0)),
            pl.BlockSpec((tile, D_MODEL), lambda i: (jnp.maximum(i - n_x, 0), 0))]


def _even_in_call(streams, mod4, layer, norm_g, w_main, w_gk, gk_bias, qn, kn, bd256, rope):
    outs = [(GLA_QK_W, BF16), (GLA_QK_W, BF16), (GLA_V_W, BF16), (GLA_V_W, BF16)]
    out_shape = [jax.ShapeDtypeStruct((NTOK, w), dt) for w, dt in outs]
    out_specs = [_row_spec(w, IN_ROW_TILE) for w, _ in outs]
    out_shape.append(jax.ShapeDtypeStruct((2, NTOK, GLA_QK_W), F32))
    out_specs.append(pl.BlockSpec((2, IN_ROW_TILE, GLA_QK_W), lambda i: (0, i, 0)))
    for w in (512, 2 * LANES):
        out_shape.append(jax.ShapeDtypeStruct((NTOK, w), BF16))
        out_specs.append(_row_spec(w, IN_ROW_TILE))
    out_shape.append(jax.ShapeDtypeStruct((LANES, NTOK), BF16))
    out_specs.append(_vt_spec())
    return pl.pallas_call(
        functools.partial(_even_in_kernel, split=len(streams) == 2),
        grid=(NTOK // IN_ROW_TILE,),
        in_specs=_stream_specs(streams, IN_ROW_TILE) + [
            _mod_spec(layer, 0, IN_ROW_TILE), _mod_spec(layer, 1, IN_ROW_TILE),
            _full_spec((1, D_MODEL)), _full_spec(w_main.shape),
            _full_spec(w_gk.shape), _full_spec(gk_bias.shape),
            _full_spec((1, LANES)), _full_spec((1, LANES)), _full_spec((2 * LANES, 2 * LANES)),
            _rope_spec(), _rope_spec(), _rope_spec(),
        ],
        out_specs=out_specs,
        out_shape=out_shape,
        compiler_params=_cparams("parallel"),
        name="even_in_proj",
    )(*streams, mod4, mod4, norm_g, w_main, w_gk, gk_bias, qn, kn, bd256, *rope)


def _odd_in_call(xs, mod4, layer, norm_g, w_in, w_layer, rope):
    widths = (C_HEADS * HEAD_DIM, 2 * LANES)
    return pl.pallas_call(
        _odd_in_kernel,
        grid=(NTOK // IN_ROW_TILE,),
        in_specs=[
            _row_spec(D_MODEL, IN_ROW_TILE), _mod_spec(layer, 0, IN_ROW_TILE),
            _mod_spec(layer, 1, IN_ROW_TILE),
            _full_spec((1, D_MODEL)),
            pl.BlockSpec((None,) + w_in.shape[1:], lambda i: (w_layer, 0, 0)),
            _rope_spec(), _rope_spec(), _rope_spec(),
        ],
        out_specs=[_row_spec(w, IN_ROW_TILE) for w in widths] + [_vt_spec()],
        out_shape=[jax.ShapeDtypeStruct((NTOK, w), BF16) for w in widths]
        + [jax.ShapeDtypeStruct((LANES, NTOK), BF16)],
        compiler_params=_cparams("parallel"),
        name="odd_in_proj",
    )(xs, mod4, mod4, norm_g, w_in, *rope)


def _attn_kernel(*refs, n_heads, window, has_sink, ctx_tiles):
    if has_sink:
        q_ref, kx_ref, vx_ref, kc_ref, vc_ref, sink_ref, o_ref = refs
    else:
        q_ref, kx_ref, vx_ref, kc_ref, vc_ref, o_ref = refs
        sink_ref = None
    tq = ATT_TQ
    t = pl.program_id(1)
    low = lax.broadcasted_iota(jnp.int32, (tq, LANES), 1) < HEAD_DIM
    zero = jnp.zeros((tq, LANES), BF16)

    def scores(hd, keys, biases):
        p, half = hd // 2, hd % 2
        kv = hd // (n_heads // 2)
        q = q_ref[:, LANES * p:LANES * (p + 1)]
        qs = jnp.where(low, q, zero) if half == 0 else jnp.where(low, zero, q)
        return [_dot_nt(k(kv), qs).astype(BF16) if b is None else _dot_nt(k(kv), qs).astype(BF16) + b
                for k, b in zip(keys, biases)]

    def finish(hd, ss, values):
        kv = hd // (n_heads // 2)
        m = functools.reduce(jnp.maximum, [jnp.max(x, axis=0, keepdims=True) for x in ss])
        if has_sink:
            sk = (sink_ref[hd] * LOG2E).astype(BF16)
            m = jnp.maximum(m, sk)
        acc = None
        for x, v in zip(ss, values):
            part = _dot(v(kv), jnp.exp2(x - m))
            acc = part if acc is None else acc + part
        den = acc[HEAD_DIM:HEAD_DIM + 1, :]
        if has_sink:
            den = den + jnp.exp2(sk - m).astype(F32)
        return acc[:HEAD_DIM, :] / den

    def all_heads(keys, values, biases):
        ahead = ATT_SCORES_AHEAD[window]
        pending = [scores(hd, keys, biases) for hd in range(ahead)]
        outs = []
        for hd in range(n_heads):
            if hd + ahead < n_heads:
                pending.append(scores(hd + ahead, keys, biases))
            outs.append(finish(hd, pending.pop(0), values))
            if hd % 2 == 1:
                ot = jnp.concatenate(outs, axis=0)
                o_ref[:, LANES * (hd // 2):LANES * (hd // 2 + 1)] = ot.T.astype(BF16)
                outs = []

    def k_cols(ref, rows=slice(None)):
        return lambda kv: ref[rows, LANES * kv:LANES * (kv + 1)]

    def v_rows(ref, cols=slice(None)):
        def get(kv):
            vt = ref[HEAD_DIM * kv:HEAD_DIM * (kv + 1), cols]
            return jnp.concatenate([vt, jnp.ones((16, vt.shape[1]), BF16)], axis=0)
        return get

    def latent_tile():
        kc, vc = k_cols(kc_ref), v_rows(vc_ref)
        if window:
            band = tq + 2 * WINDOW
            start = pl.multiple_of(jnp.clip(t * tq - WINDOW, 0, SEQ - band), LANES)
            kb = k_cols(kx_ref, pl.ds(start, band))
            vb = v_rows(vx_ref, pl.ds(start, band))
            kpos = start + lax.broadcasted_iota(jnp.int32, (band, tq), 0)
            qpos = t * tq + lax.broadcasted_iota(jnp.int32, (band, tq), 1)
            bias = jnp.where(jnp.abs(qpos - kpos) <= WINDOW, 0.0, NEG_BIG).astype(BF16)
            all_heads([kb, kc], [vb, vc], [bias, None])
        else:
            all_heads([k_cols(kx_ref), kc], [v_rows(vx_ref), vc], [None, None])

    def context_tile():
        all_heads([k_cols(kc_ref)], [v_rows(vc_ref)], [None])

    if ctx_tiles:
        pl.when(t < SEQ // tq)(latent_tile)
        pl.when(t >= SEQ // tq)(context_tile)
    else:
        latent_tile()


def _attn_call(q, k, v, sink, *, n_heads, window, ctx_tiles):
    tq = ATT_TQ
    n_xt = SEQ // tq
    n_ct = CTX_LEN // tq
    width = n_heads * HEAD_DIM
    has_sink = sink is not None

    def q_map(b, t):
        return (jnp.where(t < n_xt, b * n_xt + t, NX // tq + b * n_ct + (t - n_xt)), 0)

    in_specs = [
        pl.BlockSpec((tq, width), q_map),
        pl.BlockSpec((SEQ, 2 * LANES), lambda b, t: (b, 0)),
        pl.BlockSpec((LANES, SEQ), lambda b, t: (0, b)),
        pl.BlockSpec((CTX_LEN, 2 * LANES), lambda b, t: (NX // CTX_LEN + b, 0)),
        pl.BlockSpec((LANES, CTX_LEN), lambda b, t: (0, NX // CTX_LEN + b)),
    ]
    args = [q, k, v, k, v]
    if has_sink:
        in_specs.append(pl.BlockSpec(memory_space=pltpu.SMEM))
        args.append(sink)
    rows = NTOK if ctx_tiles else NX
    return pl.pallas_call(
        functools.partial(_attn_kernel, n_heads=n_heads, window=window, has_sink=has_sink,
                          ctx_tiles=ctx_tiles),
        grid=(BATCH, n_xt + (n_ct if ctx_tiles else 0)),
        in_specs=in_specs,
        out_specs=pl.BlockSpec((tq, width), q_map),
        out_shape=jax.ShapeDtypeStruct((rows, width), BF16),
        compiler_params=_cparams("parallel", "arbitrary"),
        name="win_attn" if window else "dense_attn",
    )(*args)


GLA_N_CTX = CTX_LEN // GLA_CHUNK
GLA_N_X = SEQ // GLA_CHUNK
GLA_STEPS = GLA_N_CTX + GLA_N_X
GLA_SAFE_LOG_DECAY = 60.0


def _gla_pos_block(b, pos):
    return jnp.where(pos < GLA_N_CTX, NX // GLA_CHUNK + b * GLA_N_CTX + pos,
                     b * GLA_N_X + pos - GLA_N_CTX)


def _gla_pos(t, fwd):
    if fwd:
        return t
    return jnp.where(t < GLA_N_CTX, GLA_N_CTX - 1 - t, GLA_STEPS + GLA_N_CTX - 1 - t)


def _bf16_pieces(x, n):
    pieces = []
    for _ in range(n):
        p = x.astype(BF16)
        pieces.append(p)
        x = x - p.astype(F32)
    return pieces


def _gla_intra_by_lag(q_ref, k_ref, v_ref, g_ref, tri, bmask_bf, oi_scr, fwd):
    C = GLA_CHUNK
    bc = sum(_dot(tri, piece) for piece in _bf16_pieces(g_ref[...], 3))
    b_pieces = _bf16_pieces(bc, 3)
    qf = q_ref[...].astype(F32)
    k, v = k_ref[...], v_ref[...]
    diff = (lax.broadcasted_iota(jnp.int32, (C, C), 0) - lax.broadcasted_iota(jnp.int32, (C, C), 1))
    if not fwd:
        diff = -diff

    def body(lag, acc):
        shift = (diff == lag).astype(F32).astype(BF16)
        bs = sum(_dot(shift, piece) for piece in b_pieces)
        w = qf * _dot(shift, k) * jnp.exp(jnp.minimum(bc - bs, 0.0))
        return acc + _dot(w.astype(BF16), bmask_bf) * _dot(shift, v)

    oi_scr[...] = lax.fori_loop(0, C, body, jnp.zeros((C, GLA_V_W), F32))


def _interleave(*chains):
    results = [None] * len(chains)
    live = list(range(len(chains)))
    while live:
        for i in list(live):
            try:
                next(chains[i])
            except StopIteration as done:
                results[i] = done.value
                live.remove(i)
    return results


def _gla_chunk(q_ref, k_ref, v_ref, g_ref, tri, bmask_ref, hmask_ref, s_scr, oi_scr, slow, fwd):
    C = GLA_CHUNK
    bc = sum(_dot(tri, piece) for piece in _bf16_pieces(g_ref[...], 2))
    yield
    tot = bc[C - 1:C, :] if fwd else bc[0:1, :]
    qf = q_ref[...].astype(F32)
    kf = k_ref[...].astype(F32)
    v = v_ref[...]
    qb = (qf * jnp.exp(bc)).astype(BF16)
    s_old = s_scr[...]
    o = _dot(qb, s_old.astype(BF16))
    kp = (kf * jnp.exp(-bc)).astype(BF16)
    qst = jnp.concatenate([qb * hmask_ref[h] for h in range(GLA_HEADS)], axis=0)
    s = _dot_nt(qst, kp).astype(BF16)
    kt = (kf * jnp.exp(tot - bc)).T.astype(BF16)
    kv = _dot(kt, v)
    yield
    low = lax.broadcasted_iota(jnp.int32, (C, LANES), 1) < GLA_DV
    parts = []
    for p in range(GLA_HEADS // 2):
        a = jnp.concatenate([s[2 * p * C:(2 * p + 1) * C] * tri, s[(2 * p + 1) * C:(2 * p + 2) * C] * tri],
                            axis=0)
        r = _dot(a, v[:, LANES * p:LANES * (p + 1)])
        parts.append(jnp.where(low, r[:C], r[C:]))
    yield
    o_intra = jnp.concatenate(parts, axis=1)
    o = o + jnp.where(slow, oi_scr[...], o_intra)
    dcol = jnp.exp(jnp.broadcast_to(tot, (LANES, GLA_QK_W))).T
    dmat = jnp.concatenate([dcol] * (GLA_V_W // LANES), axis=1)
    s_scr[...] = dmat * s_old + bmask_ref[...] * kv
    return o


def _gla_kernel(qf_ref, kf_ref, vf_ref, gf_ref, qb_ref, kb_ref, vb_ref, gb_ref, gate_ref, gn_ref,
                bmask_ref, hmask_ref, bd_ref, trif_ref, trib_ref,
                o_ref, sf_scr, sb_scr, of_scr, ob_scr, oif_scr, oib_scr):
    b = pl.program_id(0)
    t = pl.program_id(1)

    @pl.when(jnp.logical_and(b < BATCH, t == 0))
    def _():
        sf_scr[...] = jnp.zeros_like(sf_scr)
        sb_scr[...] = jnp.zeros_like(sb_scr)
        oif_scr[...] = jnp.zeros_like(oif_scr)
        oib_scr[...] = jnp.zeros_like(oib_scr)

    @pl.when(b < BATCH)
    def _():
        slow_f = jnp.min(jnp.sum(gf_ref[...], axis=0, keepdims=True)) < -GLA_SAFE_LOG_DECAY
        slow_b = jnp.min(jnp.sum(gb_ref[...], axis=0, keepdims=True)) < -GLA_SAFE_LOG_DECAY
        bmask_bf = bmask_ref[...].astype(BF16)

        @pl.when(slow_f)
        def _():
            _gla_intra_by_lag(qf_ref, kf_ref, vf_ref, gf_ref, trif_ref[...], bmask_bf, oif_scr, True)

        @pl.when(slow_b)
        def _():
            _gla_intra_by_lag(qb_ref, kb_ref, vb_ref, gb_ref, trib_ref[...], bmask_bf, oib_scr, False)

        o_f, o_b = _interleave(
            _gla_chunk(qf_ref, kf_ref, vf_ref, gf_ref, trif_ref[...], bmask_ref, hmask_ref, sf_scr,
                       oif_scr, slow_f, True),
            _gla_chunk(qb_ref, kb_ref, vb_ref, gb_ref, trib_ref[...], bmask_ref, hmask_ref, sb_scr,
                       oib_scr, slow_b, False))
        of_scr[b % 2, _gla_pos(t, True)] = o_f
        ob_scr[b % 2, _gla_pos(t, False)] = o_b

    @pl.when(b >= 1)
    def _():
        ot = of_scr[(b + 1) % 2, t] + ob_scr[(b + 1) % 2, t]
        ms = _dot((ot * ot).astype(BF16), bd_ref[...])
        y = ot * lax.rsqrt(ms + EPS) * gn_ref[...]
        gt = gate_ref[...].astype(F32)
        o_ref[...] = (y * (gt / (1.0 + jnp.exp(-gt)))).astype(BF16)


def _gla_call(qg, kg, vg, gg, dec, gn512, bmask, hmask, bd512, trif, trib):
    C = GLA_CHUNK
    last = GLA_STEPS - 1

    def scan_row_blk(b, t, fwd):
        scanning = b < BATCH
        return _gla_pos_block(jnp.minimum(b, BATCH - 1), _gla_pos(jnp.where(scanning, t, last), fwd))

    def scan_blk(fwd):
        return lambda b, t: (scan_row_blk(b, t, fwd), 0)

    def dec_blk(fwd):
        return lambda b, t: (0 if fwd else 1, scan_row_blk(b, t, fwd), 0)

    def out_blk(b, t):
        return (_gla_pos_block(jnp.maximum(b - 1, 0), jnp.where(b >= 1, t, 0)), 0)

    def chain_specs(fwd):
        return [pl.BlockSpec((C, GLA_QK_W), scan_blk(fwd)), pl.BlockSpec((C, GLA_QK_W), scan_blk(fwd)),
                pl.BlockSpec((C, GLA_V_W), scan_blk(fwd)),
                pl.BlockSpec((None, C, GLA_QK_W), dec_blk(fwd))]

    def const(shape):
        zeros = (0,) * len(shape)
        return pl.BlockSpec(shape, lambda b, t: zeros)

    return pl.pallas_call(
        _gla_kernel,
        grid=(BATCH + 1, GLA_STEPS),
        in_specs=chain_specs(True) + chain_specs(False) + [
            pl.BlockSpec((C, GLA_V_W), out_blk),
            const((1, GLA_V_W)), const(bmask.shape), const(hmask.shape), const(bd512.shape),
            const((C, C)), const((C, C)),
        ],
        out_specs=pl.BlockSpec((C, GLA_V_W), out_blk),
        out_shape=jax.ShapeDtypeStruct((NTOK, GLA_V_W), BF16),
        scratch_shapes=[pltpu.VMEM((GLA_QK_W, GLA_V_W), F32), pltpu.VMEM((GLA_QK_W, GLA_V_W), F32),
                        pltpu.VMEM((2, GLA_STEPS, C, GLA_V_W), F32),
                        pltpu.VMEM((2, GLA_STEPS, C, GLA_V_W), F32),
                        pltpu.VMEM((C, GLA_V_W), F32), pltpu.VMEM((C, GLA_V_W), F32)],
        compiler_params=_cparams("arbitrary", "arbitrary"),
        name="gla_scan",
    )(qg, kg, vg, dec, qg, kg, vg, dec, gg, gn512, bmask, hmask, bd512, trif, trib)


def _mlp_kernel(*refs, final, split):
    x_ref, xc_ref = (refs[0], refs[1]) if split else (refs[0], None)
    (ma_ref, mb_ref, wa_ref, wb_ref, g1_ref, shift_ref, scale_ref, n2_ref,
     w1_ref, w3_ref, w2_ref, g2_ref, fg_ref, o_ref) = refs[2 if split else 1:]
    hc = FFN_HIDDEN // MLP_HIDDEN_CHUNKS
    n_sub = o_ref.shape[0] // MLP_SUB_ROWS

    def mixed(r):
        rows = pl.ds(r * MLP_SUB_ROWS, MLP_SUB_ROWS)
        ox = _dot(ma_ref[rows, :], wa_ref[...]) + _dot(mb_ref[rows, :], wb_ref[...])
        x1 = _residual_rows(x_ref, xc_ref, rows) + g1_ref[...] * ox
        return x1, _norm_mod(x1, n2_ref, shift_ref, scale_ref)

    nxt = mixed(0)
    for r in range(n_sub):
        x1, h = nxt
        if r + 1 < n_sub:
            nxt = mixed(r + 1)
        y = None
        for j in range(MLP_HIDDEN_CHUNKS):
            cols = slice(j * hc, (j + 1) * hc)
            a = _dot(h, w1_ref[:, cols])
            u = ((a / (1.0 + jnp.exp(-a))) * _dot(h, w3_ref[:, cols])).astype(BF16)
            part = _dot(u, w2_ref[cols, :])
            y = part if y is None else y + part
        x2 = x1 + g2_ref[...] * y
        if final:
            ms = jnp.mean(x2 * x2, axis=-1, keepdims=True)
            x2 = x2 * lax.rsqrt(ms + EPS) * fg_ref[...]
        o_ref[pl.ds(r * MLP_SUB_ROWS, MLP_SUB_ROWS), :] = x2


def _resident_spec(stacked_shape, layer, row_block=0, n_row_blocks=1):
    _, rows, cols = stacked_shape
    return pl.BlockSpec((None, rows // n_row_blocks, cols), lambda i: (layer, row_block, 0),
                        pipeline_mode=pl.Buffered(1))


def _mlp_call(mix_a, mix_b, col_b, w_out, w_out_layer, streams, mod4, layer, norm_g, w1, w3, w2,
              final_g, rows, final):
    half = D_MODEL // 2
    tile = ROW_TILE if len(streams) == 2 else MLP_BIG_TILE
    return pl.pallas_call(
        functools.partial(_mlp_kernel, final=final, split=len(streams) == 2),
        grid=(rows // tile,),
        in_specs=_stream_specs(streams, tile) + [
            pl.BlockSpec((tile, half), lambda i: (i, 0)),
            pl.BlockSpec((tile, half), lambda i: (i, col_b)),
            _resident_spec(w_out.shape, w_out_layer, 0, 2),
            _resident_spec(w_out.shape, w_out_layer, 1, 2),
            _mod_spec(layer, 2, tile), _mod_spec(layer, 3, tile), _mod_spec(layer, 4, tile),
            _full_spec((1, D_MODEL)),
            _resident_spec(w1.shape, layer), _resident_spec(w3.shape, layer),
            _resident_spec(w2.shape, layer),
            _mod_spec(layer, 5, tile), _full_spec((1, D_MODEL)),
        ],
        out_specs=_row_spec(D_MODEL, tile),
        out_shape=jax.ShapeDtypeStruct((rows, D_MODEL), F32),
        compiler_params=_cparams("parallel"),
        name="out_proj_ffn",
    )(*streams, mix_a, mix_b, w_out, w_out, mod4, mod4, mod4, norm_g, w1, w3, w2, mod4, final_g)


def _rope_tables():
    pos = np.arange(SEQ)
    half = HEAD_DIM // 2
    inv = ROPE_THETA ** (-jnp.arange(0, half, 2, dtype=F32) / half)
    ang_r = (pos // GRID_W).astype(np.float32)[:, None] * inv[None]
    ang_c = (pos % GRID_W).astype(np.float32)[:, None] * inv[None]
    ang = jnp.concatenate([ang_r, ang_r, ang_c, ang_c] * 2, axis=-1)
    first = (np.arange(LANES) % 32) < 16
    cos, sin = jnp.cos(ang), jnp.sin(ang)
    sin_a = jnp.where(first[None], -sin, 0.0)
    sin_b = jnp.where(first[None], 0.0, sin)
    ident = jnp.ones((IN_ROW_TILE, LANES), F32)
    zero = jnp.zeros((IN_ROW_TILE, LANES), F32)
    return (jnp.concatenate([cos, ident]), jnp.concatenate([sin_a, zero]),
            jnp.concatenate([sin_b, zero]))


def _block_mean_matrix(width):
    idx = np.arange(width) // HEAD_DIM
    return jnp.asarray((idx[:, None] == idx[None, :]).astype(np.float32) / HEAD_DIM, BF16)


def kernel(x, c, ctx, c_ctx, ada_w, ada_b, norm1_g, norm2_g, ffn_w1, ffn_w3, ffn_w2,
           mixab_w_in, mixab_w_out, gla_gk_w, gla_gk_b, gla_norm_g, attn_qnorm_g, attn_knorm_g,
           win_w_in, win_w_out, win_sink, final_g):
    streams = (x.reshape(NX, D_MODEL), ctx.reshape(NC, D_MODEL))
    cvec = jnp.concatenate([c, c_ctx[None], jnp.zeros((MOD_ROWS - BATCH - 1, D_MODEL), F32)], axis=0)
    mod4 = _ada_call(cvec, ada_w, ada_b).reshape(DEPTH, MOD_ROWS, 1, 6 * D_MODEL)

    rope = _rope_tables()
    bd256 = _block_mean_matrix(2 * LANES)
    bd512 = _block_mean_matrix(GLA_V_W)
    head_of_k = np.arange(GLA_QK_W) // GLA_DK
    head_of_v = np.arange(GLA_V_W) // GLA_DV
    bmask = jnp.asarray((head_of_k[:, None] == head_of_v[None, :]).astype(np.float32))
    hmask = jnp.asarray((np.arange(GLA_HEADS)[:, None] == head_of_k[None, :]).astype(np.float32),
                        BF16).reshape(GLA_HEADS, 1, GLA_QK_W)
    tri_np = np.tril(np.ones((GLA_CHUNK, GLA_CHUNK), np.float32))
    trif, trib = jnp.asarray(tri_np, BF16), jnp.asarray(tri_np.T, BF16)
    final_row = final_g.reshape(1, D_MODEL)
    w1_b, w3_b, w2_b = ffn_w1.astype(BF16), ffn_w3.astype(BF16), ffn_w2.astype(BF16)
    even_w_out, odd_w_out = mixab_w_out.astype(BF16), win_w_out.astype(BF16)
    odd_w_in = win_w_in.astype(BF16)

    for l in range(DEPTH):
        with_ctx = l < DEPTH - 1
        i = l // 2
        n1 = norm1_g[l].reshape(1, D_MODEL)
        n2 = norm2_g[l].reshape(1, D_MODEL)
        if l % 2 == 0:
            w = mixab_w_in[i]
            w_lr = jnp.pad(w[:, 1536:1568], ((0, 0), (0, LANES - 2 * GLA_RANK)))
            w_main = jnp.concatenate([w[:, :1536], w[:, 1568:], w_lr], axis=1).astype(BF16)
            w_gk = jnp.zeros((LANES, 2 * GLA_QK_W), F32)
            w_gk = w_gk.at[:GLA_RANK, :GLA_QK_W].set(gla_gk_w[i, 0])
            w_gk = w_gk.at[GLA_RANK:2 * GLA_RANK, GLA_QK_W:].set(gla_gk_w[i, 1]).astype(BF16)
            gk_bias = gla_gk_b[i].reshape(1, 2 * GLA_QK_W)
            qn = jnp.tile(attn_qnorm_g[i], 2).reshape(1, LANES)
            kn = jnp.tile(attn_knorm_g[i], 2).reshape(1, LANES)
            qg, kg, vg, gg, dec, bq, bk, bv = _even_in_call(
                streams, mod4, l, n1, w_main, w_gk, gk_bias, qn, kn, bd256, rope)
            gn512 = jnp.tile(gla_norm_g[i], GLA_HEADS).reshape(1, GLA_V_W)
            mix_a = _gla_call(qg, kg, vg, gg, dec, gn512, bmask, hmask, bd512, trif, trib)
            mix_b = _attn_call(bq, bk, bv, None, n_heads=B_HEADS, window=False, ctx_tiles=True)
            w_out, col_b = even_w_out, 0
        else:
            q, k, v = _odd_in_call(streams[0], mod4, l, n1, odd_w_in, i, rope)
            mix_a = _attn_call(q, k, v, win_sink[i], n_heads=C_HEADS, window=True,
                               ctx_tiles=with_ctx)
            mix_b = mix_a
            w_out, col_b = odd_w_out, 1
        streams = (_mlp_call(mix_a, mix_b, col_b, w_out, i, streams, mod4, l, n2, w1_b, w3_b, w2_b,
                             final_row, NTOK if with_ctx else NX, final=not with_ctx),)
    return streams[0].reshape(BATCH, SEQ, D_MODEL)
```

```python
import functools

import numpy as np
import jax
import jax.numpy as jnp
from jax import lax
from jax.experimental import pallas as pl
from jax.experimental.pallas import tpu as pltpu

F32 = jnp.float32
BF16 = jnp.bfloat16

D_MODEL = 1024
BATCH = 8
SEQ = 2048
DEPTH = 4
GRID_W = 64
CTX_LEN = 256
HEAD_DIM = 64
ROPE_THETA = 10000.0
EPS = 1e-6
WINDOW = 128
GLA_HEADS = 8
GLA_DK = 32
GLA_DV = 64
GLA_RANK = 16
GLA_TAU = 16.0
GLA_QK_W = GLA_HEADS * GLA_DK
GLA_V_W = GLA_HEADS * GLA_DV
B_HEADS = 8
C_HEADS = 16
FFN_HIDDEN = 2816

KV_W = 2 * HEAD_DIM
B_Q_W = B_HEADS * HEAD_DIM
C_Q_W = C_HEADS * HEAD_DIM
EVEN_GK = GLA_QK_W
EVEN_GV = 2 * GLA_QK_W
EVEN_GG = EVEN_GV + GLA_V_W
EVEN_ATT = EVEN_GG + GLA_V_W
EVEN_LR_SRC = EVEN_ATT

NX = BATCH * SEQ
NC = BATCH * CTX_LEN
NTOK = NX + NC

LANES = 128
BF16_SUBLANES = 16
ROW_TILE = 512
MLP_BIG_TILE = 1024
MLP_SUB_ROWS = 512
IN_ROW_TILE = 1024
IN_SUBTILES = 2
IN_SUB_ROWS = IN_ROW_TILE // IN_SUBTILES
ATT_SCORES_AHEAD = {False: 3, True: 4}
ATT_TQ = 256
GLA_CHUNK = 256
MLP_HIDDEN_CHUNKS = 11
VMEM_LIMIT = 52 * 1024 * 1024
NEG_BIG = -1e30
LOG2E = 1.4426950408889634
Q_SCALE = HEAD_DIM ** -0.5 * LOG2E

CTX_MOD_ROW = BATCH
MOD_ROWS = 16


def _cparams(*sem):
    return pltpu.CompilerParams(dimension_semantics=sem, vmem_limit_bytes=VMEM_LIMIT)


def _dot(a, b):
    return jnp.dot(a, b, preferred_element_type=F32)


def _dot_nt(a, b):
    return lax.dot_general(a, b, (((1,), (1,)), ((), ())), preferred_element_type=F32)


def _mod_spec(layer, chunk, tile=ROW_TILE):
    def row(i):
        return jnp.where(i < NX // tile, i // (SEQ // tile), CTX_MOD_ROW)
    return pl.BlockSpec((None, None, 1, D_MODEL), lambda i: (layer, row(i), 0, chunk))


def _full_spec(shape):
    zeros = (0,) * len(shape)
    return pl.BlockSpec(shape, lambda i: zeros)


def _ada_kernel(s_ref, w_ref, b_ref, o_ref):
    s = s_ref[...]
    s = (s / (1.0 + jnp.exp(-s))).astype(BF16)
    o_ref[...] = _dot(s, w_ref[...].astype(BF16)) + b_ref[...]


def _ada_call(cvec, ada_w, ada_b):
    tn = 1536
    return pl.pallas_call(
        _ada_kernel,
        grid=(DEPTH, 6 * D_MODEL // tn),
        in_specs=[
            pl.BlockSpec((MOD_ROWS, D_MODEL), lambda l, j: (0, 0)),
            pl.BlockSpec((None, D_MODEL, tn), lambda l, j: (l, 0, j)),
            pl.BlockSpec((None, 1, tn), lambda l, j: (l, 0, j)),
        ],
        out_specs=pl.BlockSpec((None, MOD_ROWS, tn), lambda l, j: (l, 0, j)),
        out_shape=jax.ShapeDtypeStruct((DEPTH, MOD_ROWS, 6 * D_MODEL), F32),
        compiler_params=_cparams("parallel", "parallel"),
        name="ada_mod",
    )(cvec, ada_w, ada_b.reshape(DEPTH, 1, 6 * D_MODEL))


def _residual_rows(x_ref, xc_ref, rows, tile=ROW_TILE):
    if xc_ref is None:
        return x_ref[rows, :]
    return jnp.where(pl.program_id(0) < NX // tile, x_ref[rows, :], xc_ref[rows, :])


def _kv_pair_layout(a, low):
    r = pltpu.roll(a, HEAD_DIM, 1)
    return jnp.where(low, a, r), jnp.where(low, r, a)


def _store_kv(k_ref, vt_ref, rows, k, v):
    low = lax.broadcasted_iota(jnp.int32, k.shape, 1) < HEAD_DIM
    k0, k1 = _kv_pair_layout(k, low)
    k_ref[rows, 0:LANES] = k0.astype(BF16)
    k_ref[rows, LANES:2 * LANES] = k1.astype(BF16)
    vt_ref[:, rows] = v.T.astype(BF16)


def _norm_mod(x, g_ref, shift_ref, scale_ref):
    ms = jnp.mean(x * x, axis=-1, keepdims=True)
    y = x * lax.rsqrt(ms + EPS) * g_ref[...]
    return (y * (1.0 + scale_ref[...]) + shift_ref[...]).astype(BF16)


def _rope(a, cos, sin_a, sin_b):
    return (a * cos + pltpu.roll(a, LANES - 16, 1) * sin_a + pltpu.roll(a, 16, 1) * sin_b)


def _even_in_kernel(*refs, split):
    x_ref, xc_ref = (refs[0], refs[1]) if split else (refs[0], None)
    (shift_ref, scale_ref, g_ref, w_ref, wg_ref, gb_ref, qn_ref, kn_ref, bd_ref, cos_ref, sa_ref,
     sb_ref, qg_ref, kg_ref, vg_ref, gg_ref, dec_ref, bq_ref, bk_ref, bv_ref) = refs[2 if split else 1:]
    for r in range(IN_SUBTILES):
        rows = pl.ds(r * IN_SUB_ROWS, IN_SUB_ROWS)
        h = _norm_mod(_residual_rows(x_ref, xc_ref, rows, IN_ROW_TILE), g_ref, shift_ref, scale_ref)

        def proj(lo, hi):
            return _dot(h, w_ref[:, lo:hi])

        qg_ref[rows, :] = (proj(0, EVEN_GK) * GLA_DK ** -0.5).astype(BF16)
        kg_ref[rows, :] = proj(EVEN_GK, EVEN_GV).astype(BF16)
        vg_ref[rows, :] = proj(EVEN_GV, EVEN_GG).astype(BF16)
        gg_ref[rows, :] = proj(EVEN_GG, EVEN_ATT).astype(BF16)

        qk_w = B_Q_W + KV_W
        a = proj(EVEN_ATT, EVEN_ATT + qk_w + KV_W + LANES)

        lr = a[:, qk_w + KV_W:].astype(BF16)
        z = _dot(lr, wg_ref[...]) + gb_ref[...]
        logsig = jnp.minimum(z, 0.0) - jnp.log(1.0 + jnp.exp(-jnp.abs(z)))
        dec = logsig * (1.0 / GLA_TAU)
        dec_ref[0, rows, :] = dec[:, :GLA_QK_W]
        dec_ref[1, rows, :] = dec[:, GLA_QK_W:]

        sq = (a[:, :qk_w] * a[:, :qk_w]).astype(BF16)
        bd = bd_ref[...]
        ms = [_dot(sq[:, 2 * LANES * j:2 * LANES * (j + 1)], bd) for j in range(B_Q_W // (2 * LANES))]
        ms.append(_dot(sq[:, B_Q_W:qk_w], bd[:LANES, :LANES]))
        cos, sa, sb = cos_ref[rows, :], sa_ref[rows, :], sb_ref[rows, :]
        n_q_slabs = B_Q_W // LANES
        for p in range(n_q_slabs + 1):
            g_row = qn_ref[...] if p < n_q_slabs else kn_ref[...]
            msp = ms[p // 2][:, LANES * (p % 2):LANES * (p % 2 + 1)]
            n = a[:, LANES * p:LANES * (p + 1)] * lax.rsqrt(msp + EPS) * g_row
            rp = _rope(n, cos, sa, sb)
            if p < n_q_slabs:
                bq_ref[rows, LANES * p:LANES * (p + 1)] = (rp * Q_SCALE).astype(BF16)
            else:
                _store_kv(bk_ref, bv_ref, rows, rp, a[:, qk_w:qk_w + KV_W])


def _odd_in_kernel(x_ref, shift_ref, scale_ref, g_ref, w_ref, cos_ref, sa_ref, sb_ref,
                   q_ref, k_ref, v_ref):
    for r in range(IN_SUBTILES):
        rows = pl.ds(r * IN_SUB_ROWS, IN_SUB_ROWS)
        h = _norm_mod(x_ref[rows, :], g_ref, shift_ref, scale_ref)
        cos, sa, sb = cos_ref[rows, :], sa_ref[rows, :], sb_ref[rows, :]
        a = _dot(h, w_ref[...])
        for p in range(C_Q_W // LANES):
            rp = _rope(a[:, LANES * p:LANES * (p + 1)], cos, sa, sb)
            q_ref[rows, LANES * p:LANES * (p + 1)] = (rp * Q_SCALE).astype(BF16)
        _store_kv(k_ref, v_ref, rows, _rope(a[:, C_Q_W:C_Q_W + KV_W], cos, sa, sb),
                  a[:, C_Q_W + KV_W:C_Q_W + 2 * KV_W])


def _rope_spec():
    per = SEQ // IN_ROW_TILE
    return pl.BlockSpec((IN_ROW_TILE, LANES),
                        lambda i: (jnp.where(i < NX // IN_ROW_TILE, i % per, per), 0))


def _vt_spec():
    return pl.BlockSpec((LANES, IN_ROW_TILE), lambda i: (0, i))


def _row_spec(width, tile=ROW_TILE):
    return pl.BlockSpec((tile, width), lambda i: (i, 0))


def _stream_specs(streams, tile=ROW_TILE):
    if len(streams) == 1:
        return [_row_spec(D_MODEL, tile)]
    n_x = NX // tile
    return [pl.BlockSpec((tile, D_MODEL), lambda i: (jnp.minimum(i, n_x - 1), 0)),
            pl.BlockSpec((tile, D_MODEL), lambda i: (jnp.maximum(i - n_x, 0), 0))]


def _even_in_call(streams, mod4, layer, norm_g, w_main, w_gk, gk_bias, qn, kn, bd256, rope):
    outs = [(GLA_QK_W, BF16), (GLA_QK_W, BF16), (GLA_V_W, BF16), (GLA_V_W, BF16)]
    out_shape = [jax.ShapeDtypeStruct((NTOK, w), dt) for w, dt in outs]
    out_specs = [_row_spec(w, IN_ROW_TILE) for w, _ in outs]
    out_shape.append(jax.ShapeDtypeStruct((2, NTOK, GLA_QK_W), F32))
    out_specs.append(pl.BlockSpec((2, IN_ROW_TILE, GLA_QK_W), lambda i: (0, i, 0)))
    for w in (B_Q_W, 2 * KV_W):
        out_shape.append(jax.ShapeDtypeStruct((NTOK, w), BF16))
        out_specs.append(_row_spec(w, IN_ROW_TILE))
    out_shape.append(jax.ShapeDtypeStruct((LANES, NTOK), BF16))
    out_specs.append(_vt_spec())
    return pl.pallas_call(
        functools.partial(_even_in_kernel, split=len(streams) == 2),
        grid=(NTOK // IN_ROW_TILE,),
        in_specs=_stream_specs(streams, IN_ROW_TILE) + [
            _mod_spec(layer, 0, IN_ROW_TILE), _mod_spec(layer, 1, IN_ROW_TILE),
            _full_spec((1, D_MODEL)), _full_spec(w_main.shape),
            _full_spec(w_gk.shape), _full_spec(gk_bias.shape),
            _full_spec((1, LANES)), _full_spec((1, LANES)), _full_spec((2 * LANES, 2 * LANES)),
            _rope_spec(), _rope_spec(), _rope_spec(),
        ],
        out_specs=out_specs,
        out_shape=out_shape,
        compiler_params=_cparams("parallel"),
        name="even_in_proj",
    )(*streams, mod4, mod4, norm_g, w_main, w_gk, gk_bias, qn, kn, bd256, *rope)


def _odd_in_call(xs, mod4, layer, norm_g, w_in, w_layer, rope):
    widths = (C_Q_W, 2 * KV_W)
    return pl.pallas_call(
        _odd_in_kernel,
        grid=(NTOK // IN_ROW_TILE,),
        in_specs=[
            _row_spec(D_MODEL, IN_ROW_TILE), _mod_spec(layer, 0, IN_ROW_TILE),
            _mod_spec(layer, 1, IN_ROW_TILE),
            _full_spec((1, D_MODEL)),
            pl.BlockSpec((None,) + w_in.shape[1:], lambda i: (w_layer, 0, 0)),
            _rope_spec(), _rope_spec(), _rope_spec(),
        ],
        out_specs=[_row_spec(w, IN_ROW_TILE) for w in widths] + [_vt_spec()],
        out_shape=[jax.ShapeDtypeStruct((NTOK, w), BF16) for w in widths]
        + [jax.ShapeDtypeStruct((LANES, NTOK), BF16)],
        compiler_params=_cparams("parallel"),
        name="odd_in_proj",
    )(xs, mod4, mod4, norm_g, w_in, *rope)


def _attn_kernel(*refs, n_heads, window, has_sink, ctx_tiles):
    if has_sink:
        q_ref, kx_ref, vx_ref, kc_ref, vc_ref, sink_ref, o_ref = refs
    else:
        q_ref, kx_ref, vx_ref, kc_ref, vc_ref, o_ref = refs
        sink_ref = None
    tq = ATT_TQ
    t = pl.program_id(1)
    low = lax.broadcasted_iota(jnp.int32, (tq, LANES), 1) < HEAD_DIM
    zero = jnp.zeros((tq, LANES), BF16)

    def scores(hd, keys, biases):
        p, half = hd // 2, hd % 2
        kv = hd // (n_heads // 2)
        q = q_ref[:, LANES * p:LANES * (p + 1)]
        qs = jnp.where(low, q, zero) if half == 0 else jnp.where(low, zero, q)
        return [_dot_nt(k(kv), qs).astype(BF16) if b is None else _dot_nt(k(kv), qs).astype(BF16) + b
                for k, b in zip(keys, biases)]

    def finish(hd, ss, values):
        kv = hd // (n_heads // 2)
        m = functools.reduce(jnp.maximum, [jnp.max(x, axis=0, keepdims=True) for x in ss])
        if has_sink:
            sk = (sink_ref[hd] * LOG2E).astype(BF16)
            m = jnp.maximum(m, sk)
        acc = None
        for x, v in zip(ss, values):
            part = _dot(v(kv), jnp.exp2(x - m))
            acc = part if acc is None else acc + part
        den = acc[HEAD_DIM:HEAD_DIM + 1, :]
        if has_sink:
            den = den + jnp.exp2(sk - m).astype(F32)
        return acc[:HEAD_DIM, :] / den

    def all_heads(keys, values, biases):
        ahead = ATT_SCORES_AHEAD[window]
        pending = [scores(hd, keys, biases) for hd in range(ahead)]
        outs = []
        for hd in range(n_heads):
            if hd + ahead < n_heads:
                pending.append(scores(hd + ahead, keys, biases))
            outs.append(finish(hd, pending.pop(0), values))
            if hd % 2 == 1:
                ot = jnp.concatenate(outs, axis=0)
                o_ref[:, LANES * (hd // 2):LANES * (hd // 2 + 1)] = ot.T.astype(BF16)
                outs = []

    def k_cols(ref, rows=slice(None)):
        return lambda kv: ref[rows, LANES * kv:LANES * (kv + 1)]

    def v_rows(ref, cols=slice(None)):
        def get(kv):
            vt = ref[HEAD_DIM * kv:HEAD_DIM * (kv + 1), cols]
            return jnp.concatenate([vt, jnp.ones((BF16_SUBLANES, vt.shape[1]), BF16)], axis=0)
        return get

    def latent_tile():
        kc, vc = k_cols(kc_ref), v_rows(vc_ref)
        if window:
            band = tq + 2 * WINDOW
            start = pl.multiple_of(jnp.clip(t * tq - WINDOW, 0, SEQ - band), LANES)
            kb = k_cols(kx_ref, pl.ds(start, band))
            vb = v_rows(vx_ref, pl.ds(start, band))
            kpos = start + lax.broadcasted_iota(jnp.int32, (band, tq), 0)
            qpos = t * tq + lax.broadcasted_iota(jnp.int32, (band, tq), 1)
            bias = jnp.where(jnp.abs(qpos - kpos) <= WINDOW, 0.0, NEG_BIG).astype(BF16)
            all_heads([kb, kc], [vb, vc], [bias, None])
        else:
            all_heads([k_cols(kx_ref), kc], [v_rows(vx_ref), vc], [None, None])

    def context_tile():
        all_heads([k_cols(kc_ref)], [v_rows(vc_ref)], [None])

    if ctx_tiles:
        pl.when(t < SEQ // tq)(latent_tile)
        pl.when(t >= SEQ // tq)(context_tile)
    else:
        latent_tile()


def _attn_call(q, k, v, sink, *, n_heads, window, ctx_tiles):
    tq = ATT_TQ
    n_xt = SEQ // tq
    n_ct = CTX_LEN // tq
    width = n_heads * HEAD_DIM
    has_sink = sink is not None

    def q_map(b, t):
        return (jnp.where(t < n_xt, b * n_xt + t, NX // tq + b * n_ct + (t - n_xt)), 0)

    in_specs = [
        pl.BlockSpec((tq, width), q_map),
        pl.BlockSpec((SEQ, 2 * LANES), lambda b, t: (b, 0)),
        pl.BlockSpec((LANES, SEQ), lambda b, t: (0, b)),
        pl.BlockSpec((CTX_LEN, 2 * LANES), lambda b, t: (NX // CTX_LEN + b, 0)),
        pl.BlockSpec((LANES, CTX_LEN), lambda b, t: (0, NX // CTX_LEN + b)),
    ]
    args = [q, k, v, k, v]
    if has_sink:
        in_specs.append(pl.BlockSpec(memory_space=pltpu.SMEM))
        args.append(sink)
    rows = NTOK if ctx_tiles else NX
    return pl.pallas_call(
        functools.partial(_attn_kernel, n_heads=n_heads, window=window, has_sink=has_sink,
                          ctx_tiles=ctx_tiles),
        grid=(BATCH, n_xt + (n_ct if ctx_tiles else 0)),
        in_specs=in_specs,
        out_specs=pl.BlockSpec((tq, width), q_map),
        out_shape=jax.ShapeDtypeStruct((rows, width), BF16),
        compiler_params=_cparams("parallel", "arbitrary"),
        name="win_attn" if window else "dense_attn",
    )(*args)


GLA_N_CTX = CTX_LEN // GLA_CHUNK
GLA_N_X = SEQ // GLA_CHUNK
GLA_STEPS = GLA_N_CTX + GLA_N_X
GLA_SAFE_LOG_DECAY = 60.0


def _gla_pos_block(b, pos):
    return jnp.where(pos < GLA_N_CTX, NX // GLA_CHUNK + b * GLA_N_CTX + pos,
                     b * GLA_N_X + pos - GLA_N_CTX)


def _gla_pos(t, fwd):
    if fwd:
        return t
    return jnp.where(t < GLA_N_CTX, GLA_N_CTX - 1 - t, GLA_STEPS + GLA_N_CTX - 1 - t)


def _bf16_pieces(x, n):
    pieces = []
    for _ in range(n):
        p = x.astype(BF16)
        pieces.append(p)
        x = x - p.astype(F32)
    return pieces


def _gla_intra_by_lag(q_ref, k_ref, v_ref, g_ref, tri, bmask_bf, oi_scr, fwd):
    C = GLA_CHUNK
    bc = sum(_dot(tri, piece) for piece in _bf16_pieces(g_ref[...], 3))
    b_pieces = _bf16_pieces(bc, 3)
    qf = q_ref[...].astype(F32)
    k, v = k_ref[...], v_ref[...]
    diff = (lax.broadcasted_iota(jnp.int32, (C, C), 0) - lax.broadcasted_iota(jnp.int32, (C, C), 1))
    if not fwd:
        diff = -diff

    def body(lag, acc):
        shift = (diff == lag).astype(F32).astype(BF16)
        bs = sum(_dot(shift, piece) for piece in b_pieces)
        w = qf * _dot(shift, k) * jnp.exp(jnp.minimum(bc - bs, 0.0))
        return acc + _dot(w.astype(BF16), bmask_bf) * _dot(shift, v)

    oi_scr[...] = lax.fori_loop(0, C, body, jnp.zeros((C, GLA_V_W), F32))


def _interleave(*chains):
    results = [None] * len(chains)
    live = list(range(len(chains)))
    while live:
        for i in list(live):
            try:
                next(chains[i])
            except StopIteration as done:
                results[i] = done.value
                live.remove(i)
    return results


def _gla_chunk(q_ref, k_ref, v_ref, g_ref, tri, bmask_ref, hmask_ref, s_scr, oi_scr, slow, fwd):
    C = GLA_CHUNK
    bc = sum(_dot(tri, piece) for piece in _bf16_pieces(g_ref[...], 2))
    yield
    tot = bc[C - 1:C, :] if fwd else bc[0:1, :]
    qf = q_ref[...].astype(F32)
    kf = k_ref[...].astype(F32)
    v = v_ref[...]
    qb = (qf * jnp.exp(bc)).astype(BF16)
    s_old = s_scr[...]
    o = _dot(qb, s_old.astype(BF16))
    kp = (kf * jnp.exp(-bc)).astype(BF16)
    qst = jnp.concatenate([qb * hmask_ref[h] for h in range(GLA_HEADS)], axis=0)
    s = _dot_nt(qst, kp).astype(BF16)
    kt = (kf * jnp.exp(tot - bc)).T.astype(BF16)
    kv = _dot(kt, v)
    yield
    low = lax.broadcasted_iota(jnp.int32, (C, LANES), 1) < GLA_DV
    parts = []
    for p in range(GLA_HEADS // 2):
        a = jnp.concatenate([s[2 * p * C:(2 * p + 1) * C] * tri, s[(2 * p + 1) * C:(2 * p + 2) * C] * tri],
                            axis=0)
        r = _dot(a, v[:, LANES * p:LANES * (p + 1)])
        parts.append(jnp.where(low, r[:C], r[C:]))
    yield
    o_intra = jnp.concatenate(parts, axis=1)
    o = o + jnp.where(slow, oi_scr[...], o_intra)
    dcol = jnp.exp(jnp.broadcast_to(tot, (LANES, GLA_QK_W))).T
    dmat = jnp.concatenate([dcol] * (GLA_V_W // LANES), axis=1)
    s_scr[...] = dmat * s_old + bmask_ref[...] * kv
    return o


def _gla_kernel(qf_ref, kf_ref, vf_ref, gf_ref, qb_ref, kb_ref, vb_ref, gb_ref, gate_ref, gn_ref,
                bmask_ref, hmask_ref, bd_ref, trif_ref, trib_ref,
                o_ref, sf_scr, sb_scr, of_scr, ob_scr, oif_scr, oib_scr):
    b = pl.program_id(0)
    t = pl.program_id(1)

    @pl.when(jnp.logical_and(b < BATCH, t == 0))
    def _():
        sf_scr[...] = jnp.zeros_like(sf_scr)
        sb_scr[...] = jnp.zeros_like(sb_scr)
        oif_scr[...] = jnp.zeros_like(oif_scr)
        oib_scr[...] = jnp.zeros_like(oib_scr)

    @pl.when(b < BATCH)
    def _():
        slow_f = jnp.min(jnp.sum(gf_ref[...], axis=0, keepdims=True)) < -GLA_SAFE_LOG_DECAY
        slow_b = jnp.min(jnp.sum(gb_ref[...], axis=0, keepdims=True)) < -GLA_SAFE_LOG_DECAY
        bmask_bf = bmask_ref[...].astype(BF16)

        @pl.when(slow_f)
        def _():
            _gla_intra_by_lag(qf_ref, kf_ref, vf_ref, gf_ref, trif_ref[...], bmask_bf, oif_scr, True)

        @pl.when(slow_b)
        def _():
            _gla_intra_by_lag(qb_ref, kb_ref, vb_ref, gb_ref, trib_ref[...], bmask_bf, oib_scr, False)

        o_f, o_b = _interleave(
            _gla_chunk(qf_ref, kf_ref, vf_ref, gf_ref, trif_ref[...], bmask_ref, hmask_ref, sf_scr,
                       oif_scr, slow_f, True),
            _gla_chunk(qb_ref, kb_ref, vb_ref, gb_ref, trib_ref[...], bmask_ref, hmask_ref, sb_scr,
                       oib_scr, slow_b, False))
        of_scr[b % 2, _gla_pos(t, True)] = o_f
        ob_scr[b % 2, _gla_pos(t, False)] = o_b

    @pl.when(b >= 1)
    def _():
        ot = of_scr[(b + 1) % 2, t] + ob_scr[(b + 1) % 2, t]
        ms = _dot((ot * ot).astype(BF16), bd_ref[...])
        y = ot * lax.rsqrt(ms + EPS) * gn_ref[...]
        gt = gate_ref[...].astype(F32)
        o_ref[...] = (y * (gt / (1.0 + jnp.exp(-gt)))).astype(BF16)


def _gla_call(qg, kg, vg, gg, dec, gn512, bmask, hmask, bd512, trif, trib):
    C = GLA_CHUNK
    last = GLA_STEPS - 1

    def scan_row_blk(b, t, fwd):
        scanning = b < BATCH
        return _gla_pos_block(jnp.minimum(b, BATCH - 1), _gla_pos(jnp.where(scanning, t, last), fwd))

    def scan_blk(fwd):
        return lambda b, t: (scan_row_blk(b, t, fwd), 0)

    def dec_blk(fwd):
        return lambda b, t: (0 if fwd else 1, scan_row_blk(b, t, fwd), 0)

    def out_blk(b, t):
        return (_gla_pos_block(jnp.maximum(b - 1, 0), jnp.where(b >= 1, t, 0)), 0)

    def chain_specs(fwd):
        return [pl.BlockSpec((C, GLA_QK_W), scan_blk(fwd)), pl.BlockSpec((C, GLA_QK_W), scan_blk(fwd)),
                pl.BlockSpec((C, GLA_V_W), scan_blk(fwd)),
                pl.BlockSpec((None, C, GLA_QK_W), dec_blk(fwd))]

    def const(shape):
        zeros = (0,) * len(shape)
        return pl.BlockSpec(shape, lambda b, t: zeros)

    return pl.pallas_call(
        _gla_kernel,
        grid=(BATCH + 1, GLA_STEPS),
        in_specs=chain_specs(True) + chain_specs(False) + [
            pl.BlockSpec((C, GLA_V_W), out_blk),
            const((1, GLA_V_W)), const(bmask.shape), const(hmask.shape), const(bd512.shape),
            const((C, C)), const((C, C)),
        ],
        out_specs=pl.BlockSpec((C, GLA_V_W), out_blk),
        out_shape=jax.ShapeDtypeStruct((NTOK, GLA_V_W), BF16),
        scratch_shapes=[pltpu.VMEM((GLA_QK_W, GLA_V_W), F32), pltpu.VMEM((GLA_QK_W, GLA_V_W), F32),
                        pltpu.VMEM((2, GLA_STEPS, C, GLA_V_W), F32),
                        pltpu.VMEM((2, GLA_STEPS, C, GLA_V_W), F32),
                        pltpu.VMEM((C, GLA_V_W), F32), pltpu.VMEM((C, GLA_V_W), F32)],
        compiler_params=_cparams("arbitrary", "arbitrary"),
        name="gla_scan",
    )(qg, kg, vg, dec, qg, kg, vg, dec, gg, gn512, bmask, hmask, bd512, trif, trib)


def _mlp_kernel(*refs, final, split):
    x_ref, xc_ref = (refs[0], refs[1]) if split else (refs[0], None)
    (ma_ref, mb_ref, wa_ref, wb_ref, g1_ref, shift_ref, scale_ref, n2_ref,
     w1_ref, w3_ref, w2_ref, g2_ref, fg_ref, o_ref) = refs[2 if split else 1:]
    hc = FFN_HIDDEN // MLP_HIDDEN_CHUNKS
    n_sub = o_ref.shape[0] // MLP_SUB_ROWS

    def mixed(r):
        rows = pl.ds(r * MLP_SUB_ROWS, MLP_SUB_ROWS)
        ox = _dot(ma_ref[rows, :], wa_ref[...]) + _dot(mb_ref[rows, :], wb_ref[...])
        x1 = _residual_rows(x_ref, xc_ref, rows) + g1_ref[...] * ox
        return x1, _norm_mod(x1, n2_ref, shift_ref, scale_ref)

    nxt = mixed(0)
    for r in range(n_sub):
        x1, h = nxt
        if r + 1 < n_sub:
            nxt = mixed(r + 1)
        y = None
        for j in range(MLP_HIDDEN_CHUNKS):
            cols = slice(j * hc, (j + 1) * hc)
            a = _dot(h, w1_ref[:, cols])
            u = ((a / (1.0 + jnp.exp(-a))) * _dot(h, w3_ref[:, cols])).astype(BF16)
            part = _dot(u, w2_ref[cols, :])
            y = part if y is None else y + part
        x2 = x1 + g2_ref[...] * y
        if final:
            ms = jnp.mean(x2 * x2, axis=-1, keepdims=True)
            x2 = x2 * lax.rsqrt(ms + EPS) * fg_ref[...]
        o_ref[pl.ds(r * MLP_SUB_ROWS, MLP_SUB_ROWS), :] = x2


def _resident_spec(stacked_shape, layer, row_block=0, n_row_blocks=1):
    _, rows, cols = stacked_shape
    return pl.BlockSpec((None, rows // n_row_blocks, cols), lambda i: (layer, row_block, 0),
                        pipeline_mode=pl.Buffered(1))


def _mlp_call(mix_a, mix_b, col_b, w_out, w_out_layer, streams, mod4, layer, norm_g, w1, w3, w2,
              final_g, rows, final):
    half = D_MODEL // 2
    tile = ROW_TILE if len(streams) == 2 else MLP_BIG_TILE
    return pl.pallas_call(
        functools.partial(_mlp_kernel, final=final, split=len(streams) == 2),
        grid=(rows // tile,),
        in_specs=_stream_specs(streams, tile) + [
            pl.BlockSpec((tile, half), lambda i: (i, 0)),
            pl.BlockSpec((tile, half), lambda i: (i, col_b)),
            _resident_spec(w_out.shape, w_out_layer, 0, 2),
            _resident_spec(w_out.shape, w_out_layer, 1, 2),
            _mod_spec(layer, 2, tile), _mod_spec(layer, 3, tile), _mod_spec(layer, 4, tile),
            _full_spec((1, D_MODEL)),
            _resident_spec(w1.shape, layer), _resident_spec(w3.shape, layer),
            _resident_spec(w2.shape, layer),
            _mod_spec(layer, 5, tile), _full_spec((1, D_MODEL)),
        ],
        out_specs=_row_spec(D_MODEL, tile),
        out_shape=jax.ShapeDtypeStruct((rows, D_MODEL), F32),
        compiler_params=_cparams("parallel"),
        name="out_proj_ffn",
    )(*streams, mix_a, mix_b, w_out, w_out, mod4, mod4, mod4, norm_g, w1, w3, w2, mod4, final_g)


def _rope_tables():
    pos = np.arange(SEQ)
    half = HEAD_DIM // 2
    inv = ROPE_THETA ** (-jnp.arange(0, half, 2, dtype=F32) / half)
    ang_r = (pos // GRID_W).astype(np.float32)[:, None] * inv[None]
    ang_c = (pos % GRID_W).astype(np.float32)[:, None] * inv[None]
    ang = jnp.concatenate([ang_r, ang_r, ang_c, ang_c] * 2, axis=-1)
    first = (np.arange(LANES) % 32) < 16
    cos, sin = jnp.cos(ang), jnp.sin(ang)
    sin_a = jnp.where(first[None], -sin, 0.0)
    sin_b = jnp.where(first[None], 0.0, sin)
    ident = jnp.ones((IN_ROW_TILE, LANES), F32)
    zero = jnp.zeros((IN_ROW_TILE, LANES), F32)
    return (jnp.concatenate([cos, ident]), jnp.concatenate([sin_a, zero]),
            jnp.concatenate([sin_b, zero]))


def _block_mean_matrix(width):
    idx = np.arange(width) // HEAD_DIM
    return jnp.asarray((idx[:, None] == idx[None, :]).astype(np.float32) / HEAD_DIM, BF16)


def kernel(x, c, ctx, c_ctx, ada_w, ada_b, norm1_g, norm2_g, ffn_w1, ffn_w3, ffn_w2,
           mixab_w_in, mixab_w_out, gla_gk_w, gla_gk_b, gla_norm_g, attn_qnorm_g, attn_knorm_g,
           win_w_in, win_w_out, win_sink, final_g):
    streams = (x.reshape(NX, D_MODEL), ctx.reshape(NC, D_MODEL))
    cvec = jnp.concatenate([c, c_ctx[None], jnp.zeros((MOD_ROWS - BATCH - 1, D_MODEL), F32)], axis=0)
    mod4 = _ada_call(cvec, ada_w, ada_b).reshape(DEPTH, MOD_ROWS, 1, 6 * D_MODEL)

    rope = _rope_tables()
    bd256 = _block_mean_matrix(2 * LANES)
    bd512 = _block_mean_matrix(GLA_V_W)
    head_of_k = np.arange(GLA_QK_W) // GLA_DK
    head_of_v = np.arange(GLA_V_W) // GLA_DV
    bmask = jnp.asarray((head_of_k[:, None] == head_of_v[None, :]).astype(np.float32))
    hmask = jnp.asarray((np.arange(GLA_HEADS)[:, None] == head_of_k[None, :]).astype(np.float32),
                        BF16).reshape(GLA_HEADS, 1, GLA_QK_W)
    tri_np = np.tril(np.ones((GLA_CHUNK, GLA_CHUNK), np.float32))
    trif, trib = jnp.asarray(tri_np, BF16), jnp.asarray(tri_np.T, BF16)
    final_row = final_g.reshape(1, D_MODEL)
    w1_b, w3_b, w2_b = ffn_w1.astype(BF16), ffn_w3.astype(BF16), ffn_w2.astype(BF16)
    even_w_out, odd_w_out = mixab_w_out.astype(BF16), win_w_out.astype(BF16)
    odd_w_in = win_w_in.astype(BF16)

    for l in range(DEPTH):
        with_ctx = l < DEPTH - 1
        i = l // 2
        n1 = norm1_g[l].reshape(1, D_MODEL)
        n2 = norm2_g[l].reshape(1, D_MODEL)
        if l % 2 == 0:
            w = mixab_w_in[i]
            lr_end = EVEN_LR_SRC + 2 * GLA_RANK
            w_lr = jnp.pad(w[:, EVEN_LR_SRC:lr_end], ((0, 0), (0, LANES - 2 * GLA_RANK)))
            w_main = jnp.concatenate([w[:, :EVEN_LR_SRC], w[:, lr_end:], w_lr], axis=1).astype(BF16)
            w_gk = jnp.zeros((LANES, 2 * GLA_QK_W), F32)
            w_gk = w_gk.at[:GLA_RANK, :GLA_QK_W].set(gla_gk_w[i, 0])
            w_gk = w_gk.at[GLA_RANK:2 * GLA_RANK, GLA_QK_W:].set(gla_gk_w[i, 1]).astype(BF16)
            gk_bias = gla_gk_b[i].reshape(1, 2 * GLA_QK_W)
            qn = jnp.tile(attn_qnorm_g[i], 2).reshape(1, LANES)
            kn = jnp.tile(attn_knorm_g[i], 2).reshape(1, LANES)
            qg, kg, vg, gg, dec, bq, bk, bv = _even_in_call(
                streams, mod4, l, n1, w_main, w_gk, gk_bias, qn, kn, bd256, rope)
            gn512 = jnp.tile(gla_norm_g[i], GLA_HEADS).reshape(1, GLA_V_W)
            mix_a = _gla_call(qg, kg, vg, gg, dec, gn512, bmask, hmask, bd512, trif, trib)
            mix_b = _attn_call(bq, bk, bv, None, n_heads=B_HEADS, window=False, ctx_tiles=True)
            w_out, col_b = even_w_out, 0
        else:
            q, k, v = _odd_in_call(streams[0], mod4, l, n1, odd_w_in, i, rope)
            mix_a = _attn_call(q, k, v, win_sink[i], n_heads=C_HEADS, window=True,
                               ctx_tiles=with_ctx)
            mix_b = mix_a
            w_out, col_b = odd_w_out, 1
        streams = (_mlp_call(mix_a, mix_b, col_b, w_out, i, streams, mod4, l, n2, w1_b, w3_b, w2_b,
                             final_row, NTOK if with_ctx else NX, final=not with_ctx),)
    return streams[0].reshape(BATCH, SEQ, D_MODEL)
```
